```python
import math
import jax
import jax.numpy as jnp
from jax import lax
import numpy as np

D_MODEL = 2048
BATCH = 1
SEQ = 8192
DEPTH = 4

S5_GROUP_CH = 16
S5_STATE = 64
S5_WIDTH = D_MODEL
S5_GROUPS = S5_WIDTH // S5_GROUP_CH
DT_MIN = 1e-3
DT_MAX = 1e-1
DN_QK_HEADS = 16
DN_V_HEADS = 32
DN_HEAD_DIM = 128
DN_KEY_WIDTH = DN_QK_HEADS * DN_HEAD_DIM
DN_VALUE_WIDTH = DN_V_HEADS * DN_HEAD_DIM
DN_CONV_CH = 2 * DN_KEY_WIDTH + DN_VALUE_WIDTH
DN_CONV = 4
DN_CHUNK = 64
N_EXPERTS = 32
TOP_K = 4
D_EXPERT = 512
SWIGLU_LIMIT = 7.0
SWIGLU_ALPHA = 1.702
MOE_BLOCK = 128
DEEPNORM_ALPHA = (2 * DEPTH) ** 0.25
DEEPNORM_BETA = (8 * DEPTH) ** -0.25
LN_EPS = 1e-5
RMS_EPS = 1e-6
L2_EPS = 1e-6
S5_MAX_REAL = -1e-4
OFF_S5 = 0
OFF_QKV = OFF_S5 + S5_WIDTH
OFF_Z = OFF_QKV + DN_CONV_CH
OFF_BETA = OFF_Z + DN_VALUE_WIDTH
OFF_DECAY = OFF_BETA + DN_V_HEADS
OFF_GATE_S5 = OFF_DECAY + DN_V_HEADS
OFF_GATE_DN = OFF_GATE_S5 + D_MODEL
IN_COLS = OFF_GATE_DN + D_MODEL

kernel_name = 'hybrid_s5_gdn_moe_deepnorm'


def layer_norm(x, g, b):
    xf = x.astype(jnp.float32)
    mu = jnp.mean(xf, axis=-1, keepdims=True)
    var = jnp.mean(jnp.square(xf - mu), axis=-1, keepdims=True)
    y = (xf - mu) * lax.rsqrt(var + LN_EPS) * g.astype(jnp.float32) + b.astype(jnp.float32)
    return y.astype(x.dtype)


def l2_normalize(t):
    return t * lax.rsqrt(jnp.sum(jnp.square(t), axis=-1, keepdims=True) + L2_EPS)


def causal_depthwise_conv(x, w):
    T = x.shape[1]
    K = w.shape[1]
    xp = jnp.pad(x, ((0, 0), (K - 1, 0), (0, 0)))
    out = xp[:, 0:T, :] * w[:, 0]
    for j in range(1, K):
        out = out + xp[:, j:j + T, :] * w[:, j]
    return out


def s5_branch(u, lam_re, lam_im, log_dt, b_re, b_im, c_re, c_im, d_skip, w_glu_a, w_glu_b):
    f32 = jnp.float32
    bsz, T, _ = u.shape
    u = u.astype(f32).reshape(bsz, T, S5_GROUPS, S5_GROUP_CH)
    lre = jnp.minimum(lam_re.astype(f32), S5_MAX_REAL)
    lim = lam_im.astype(f32)
    dt = jnp.exp(log_dt.astype(f32))[:, None]
    mag = jnp.exp(lre * dt)
    a_re = mag * jnp.cos(lim * dt)
    a_im = mag * jnp.sin(lim * dt)
    den = jnp.square(lre) + jnp.square(lim)
    f_re = ((a_re - 1.0) * lre + a_im * lim) / den
    f_im = (a_im * lre - (a_re - 1.0) * lim) / den
    br = b_re.astype(f32)
    bi = b_im.astype(f32)
    bb_re = f_re[..., None] * br - f_im[..., None] * bi
    bb_im = f_re[..., None] * bi + f_im[..., None] * br
    bu_re = jnp.einsum('btgp,gnp->btgn', u, bb_re)
    bu_im = jnp.einsum('btgp,gnp->btgn', u, bb_im)
    lam_bar_re = jnp.broadcast_to(a_re, bu_re.shape)
    lam_bar_im = jnp.broadcast_to(a_im, bu_im.shape)

    def combine(earlier, later):
        ar1, ai1, br1, bi1 = earlier
        ar2, ai2, br2, bi2 = later
        return (ar2 * ar1 - ai2 * ai1,
                ar2 * ai1 + ai2 * ar1,
                ar2 * br1 - ai2 * bi1 + br2,
                ar2 * bi1 + ai2 * br1 + bi2)

    _, _, s_re, s_im = lax.associative_scan(
        combine, (lam_bar_re, lam_bar_im, bu_re, bu_im), axis=1)
    y = (jnp.einsum('btgn,gpn->btgp', s_re, c_re.astype(f32))
         - jnp.einsum('btgn,gpn->btgp', s_im, c_im.astype(f32))
         + d_skip.astype(f32) * u)
    y = jax.nn.gelu(y.reshape(bsz, T, S5_WIDTH))
    return (y @ w_glu_a.astype(f32)) * jax.nn.sigmoid(y @ w_glu_b.astype(f32))


def chunked_gated_delta_rule(q, k, v, g, beta):
    bsz, T, H, dk = q.shape
    dv = v.shape[-1]
    C = DN_CHUNK
    n = T // C
    q = q * (dk ** -0.5)

    def to_chunks(t):
        return t.reshape(bsz, n, C, H, -1).transpose(0, 3, 1, 2, 4)

    q, k, v = to_chunks(q), to_chunks(k), to_chunks(v)
    g = g.reshape(bsz, n, C, H).transpose(0, 3, 1, 2)
    beta = beta.reshape(bsz, n, C, H).transpose(0, 3, 1, 2)
    g = jnp.cumsum(g, axis=-1)
    incl = jnp.tril(jnp.ones((C, C), dtype=bool))
    strict = jnp.tril(jnp.ones((C, C), dtype=bool), -1)
    decay = jnp.exp(jnp.where(incl, g[..., :, None] - g[..., None, :], -jnp.inf))
    kb = k * beta[..., None]
    a_mat = jnp.where(strict, jnp.einsum('bhncd,bhnsd->bhncs', kb, k) * decay, 0.0)
    lower = a_mat + jnp.eye(C, dtype=a_mat.dtype)
    rhs = jnp.concatenate([v * beta[..., None], kb * jnp.exp(g)[..., None]], axis=-1)
    sol = lax.linalg.triangular_solve(lower, rhs, left_side=True, lower=True, unit_diagonal=True)
    u_chunk = sol[..., :dv]
    w_chunk = sol[..., dv:]
    attn_intra = jnp.where(incl, jnp.einsum('bhncd,bhnsd->bhncs', q, k) * decay, 0.0)
    q_dec = q * jnp.exp(g)[..., None]
    k_dec = k * jnp.exp(g[..., -1:] - g)[..., None]
    g_last = jnp.exp(g[..., -1])

    def step(S, xs):
        qd, wc, uc, at, kd, gl = xs
        v_new = uc - jnp.einsum('bhcd,bhde->bhce', wc, S)
        o = jnp.einsum('bhcd,bhde->bhce', qd, S) + jnp.einsum('bhcs,bhse->bhce', at, v_new)
        S = S * gl[..., None, None] + jnp.einsum('bhcd,bhce->bhde', kd, v_new)
        return S, o

    xs = (jnp.moveaxis(q_dec, 2, 0), jnp.moveaxis(w_chunk, 2, 0), jnp.moveaxis(u_chunk, 2, 0),
          jnp.moveaxis(attn_intra, 2, 0), jnp.moveaxis(k_dec, 2, 0), jnp.moveaxis(g_last, 2, 0))
    S0 = jnp.zeros((bsz, H, dk, dv), jnp.float32)
    _, o = lax.scan(step, S0, xs)
    return o.transpose(1, 0, 3, 2, 4).reshape(bsz, T, H, dv)


def deltanet_branch(qkv, z, beta_logit, decay_logit, conv_w, a_log, dt_bias, norm_w, w_dn_out):
    f32 = jnp.float32
    bsz, T, _ = qkv.shape
    qkv = jax.nn.silu(causal_depthwise_conv(qkv.astype(f32), conv_w.astype(f32)))
    q = l2_normalize(qkv[..., :DN_KEY_WIDTH].reshape(bsz, T, DN_QK_HEADS, DN_HEAD_DIM))
    k = l2_normalize(qkv[..., DN_KEY_WIDTH:2 * DN_KEY_WIDTH].reshape(bsz, T, DN_QK_HEADS, DN_HEAD_DIM))
    v = qkv[..., 2 * DN_KEY_WIDTH:].reshape(bsz, T, DN_V_HEADS, DN_HEAD_DIM)
    rep = DN_V_HEADS // DN_QK_HEADS
    q = jnp.repeat(q, rep, axis=2)
    k = jnp.repeat(k, rep, axis=2)
    beta = jax.nn.sigmoid(beta_logit.astype(f32))
    g = -jnp.exp(a_log.astype(f32)) * jax.nn.softplus(decay_logit.astype(f32) + dt_bias.astype(f32))
    o = chunked_gated_delta_rule(q, k, v, g, beta)
    o = o * lax.rsqrt(jnp.mean(jnp.square(o), axis=-1, keepdims=True) + RMS_EPS)
    o = o * norm_w.astype(f32) * jax.nn.silu(z.astype(f32).reshape(bsz, T, DN_V_HEADS, DN_HEAD_DIM))
    return o.reshape(bsz, T, DN_VALUE_WIDTH) @ w_dn_out.astype(f32)


def token_mixer(x, w_in, dn_conv_w, dn_a_log, dn_dt_bias, dn_norm_w, w_dn_out,
                s5_lam_re, s5_lam_im, s5_log_dt, s5_b_re, s5_b_im, s5_c_re, s5_c_im, s5_d,
                w_glu_a, w_glu_b, w_mix_out):
    proj = x @ w_in
    y_s5 = s5_branch(proj[..., OFF_S5:OFF_QKV], s5_lam_re, s5_lam_im, s5_log_dt,
                     s5_b_re, s5_b_im, s5_c_re, s5_c_im, s5_d, w_glu_a, w_glu_b)
    y_dn = deltanet_branch(proj[..., OFF_QKV:OFF_Z], proj[..., OFF_Z:OFF_BETA],
                           proj[..., OFF_BETA:OFF_DECAY], proj[..., OFF_DECAY:OFF_GATE_S5],
                           dn_conv_w, dn_a_log, dn_dt_bias, dn_norm_w, w_dn_out)
    gate_s5 = jax.nn.sigmoid(proj[..., OFF_GATE_S5:OFF_GATE_DN].astype(jnp.float32))
    gate_dn = jax.nn.sigmoid(proj[..., OFF_GATE_DN:IN_COLS].astype(jnp.float32))
    merged = gate_s5 * y_s5 + gate_dn * y_dn
    return (merged @ w_mix_out.astype(jnp.float32)).astype(x.dtype)


def routed_experts(x, w_router, b_router, w_gate, b_gate, w_up, b_up, w_down, b_down):
    bsz, T_seq, D = x.shape
    T = bsz * T_seq
    x2 = x.reshape(T, D)
    logits = x2.astype(jnp.float32) @ w_router.astype(jnp.float32) + b_router.astype(jnp.float32)
    top_val, top_idx = lax.top_k(logits, TOP_K)
    top_w = jax.nn.softmax(top_val, axis=-1)
    TK = T * TOP_K
    flat_e = top_idx.reshape(-1)
    flat_tok = jnp.repeat(jnp.arange(T, dtype=jnp.int32), TOP_K)
    flat_w = top_w.reshape(-1)
    order = jnp.argsort(flat_e)
    se, stok, sw = flat_e[order], flat_tok[order], flat_w[order]
    counts = jnp.bincount(flat_e, length=N_EXPERTS)
    padded = (counts + MOE_BLOCK - 1) // MOE_BLOCK * MOE_BLOCK
    start = jnp.cumsum(counts) - counts
    pend = jnp.cumsum(padded)
    pstart = pend - padded
    dest = pstart[se] + (jnp.arange(TK, dtype=jnp.int32) - start[se])
    n_blocks = -(-TK // MOE_BLOCK) + N_EXPERTS
    n_slots = n_blocks * MOE_BLOCK
    slot_tok = jnp.full((n_slots,), T, jnp.int32).at[dest].set(stok)
    slot_w = jnp.zeros((n_slots,), jnp.float32).at[dest].set(sw)
    block_e = jnp.minimum(jnp.searchsorted(pend, jnp.arange(n_blocks, dtype=jnp.int32) * MOE_BLOCK,
                                           side='right'), N_EXPERTS - 1)
    x_pad = jnp.concatenate([x2, jnp.zeros((1, D), x2.dtype)], axis=0)

    def expert_block(args):
        tok, e = args
        xb = x_pad[tok]
        gate = jnp.minimum(xb @ w_gate[e] + b_gate[e], SWIGLU_LIMIT)
        up = jnp.clip(xb @ w_up[e] + b_up[e], -SWIGLU_LIMIT, SWIGLU_LIMIT)
        h = gate * jax.nn.sigmoid(SWIGLU_ALPHA * gate) * (up + 1.0)
        return h @ w_down[e] + b_down[e]

    y = lax.map(expert_block, (slot_tok.reshape(n_blocks, MOE_BLOCK), block_e))
    y = y.reshape(n_slots, D) * slot_w[:, None]
    out = jax.ops.segment_sum(y, slot_tok, num_segments=T + 1)[:T]
    return out.reshape(bsz, T_seq, D).astype(x.dtype)


def setup_inputs(seed: int = 0) -> dict:
    key = jax.random.key(seed)
    ks = jax.random.split(key, 32)
    f32 = jnp.float32
    L, D = DEPTH, D_MODEL

    def normal(k, shape, scale):
        return jax.random.normal(k, shape, f32) * scale

    def uniform(k, shape, lo, hi):
        return jax.random.uniform(k, shape, f32, lo, hi)

    x = normal(ks[0], (BATCH, SEQ, D), 1.0)
    w_in = normal(ks[1], (L, D, IN_COLS), D ** -0.5)
    dn_conv_w = normal(ks[2], (L, DN_CONV_CH, DN_CONV), DN_CONV ** -0.5)
    dn_a_log = jnp.log(uniform(ks[3], (L, DN_V_HEADS), 1.0, 16.0))
    dn_dt = jnp.exp(uniform(ks[4], (L, DN_V_HEADS), math.log(DT_MIN), math.log(DT_MAX)))
    dn_dt_bias = dn_dt + jnp.log(-jnp.expm1(-dn_dt))
    dn_norm_w = 1.0 + normal(ks[5], (L, DN_HEAD_DIM), 0.02)
    w_dn_out = normal(ks[6], (L, DN_VALUE_WIDTH, D), DN_VALUE_WIDTH ** -0.5)
    s5_lam_re = -0.5 + normal(ks[7], (L, S5_GROUPS, S5_STATE), 0.01)
    s5_lam_im = math.pi * jnp.arange(S5_STATE, dtype=f32) + normal(ks[8], (L, S5_GROUPS, S5_STATE), 0.01)
    s5_log_dt = uniform(ks[9], (L, S5_GROUPS), math.log(DT_MIN), math.log(DT_MAX))
    s5_b_re = normal(ks[10], (L, S5_GROUPS, S5_STATE, S5_GROUP_CH), (2 * S5_GROUP_CH) ** -0.5)
    s5_b_im = normal(ks[11], (L, S5_GROUPS, S5_STATE, S5_GROUP_CH), (2 * S5_GROUP_CH) ** -0.5)
    s5_c_re = normal(ks[12], (L, S5_GROUPS, S5_GROUP_CH, S5_STATE), S5_STATE ** -0.5)
    s5_c_im = normal(ks[13], (L, S5_GROUPS, S5_GROUP_CH, S5_STATE), S5_STATE ** -0.5)
    s5_d = normal(ks[14], (L, S5_GROUPS, S5_GROUP_CH), 1.0)
    w_glu_a = normal(ks[15], (L, S5_WIDTH, D), S5_WIDTH ** -0.5)
    w_glu_b = normal(ks[16], (L, S5_WIDTH, D), S5_WIDTH ** -0.5)
    w_mix_out = normal(ks[17], (L, D, D), D ** -0.5 * DEEPNORM_BETA)
    ln1_g = 1.0 + normal(ks[18], (L, D), 0.02)
    ln1_b = normal(ks[19], (L, D), 0.02)
    w_router = normal(ks[20], (L, D, N_EXPERTS), D ** -0.5)
    b_router = normal(ks[21], (L, N_EXPERTS), 0.01)
    w_gate = normal(ks[22], (L, N_EXPERTS, D, D_EXPERT), D ** -0.5)
    b_gate = normal(ks[23], (L, N_EXPERTS, D_EXPERT), 0.02)
    w_up = normal(ks[24], (L, N_EXPERTS, D, D_EXPERT), D ** -0.5)
    b_up = normal(ks[25], (L, N_EXPERTS, D_EXPERT), 0.02)
    w_down = normal(ks[26], (L, N_EXPERTS, D_EXPERT, D), D_EXPERT ** -0.5 * DEEPNORM_BETA)
    b_down = normal(ks[27], (L, N_EXPERTS, D), 0.02)
    ln2_g = 1.0 + normal(ks[28], (L, D), 0.02)
    ln2_b = normal(ks[29], (L, D), 0.02)
    return {'x': x, 'w_in': w_in, 'dn_conv_w': dn_conv_w, 'dn_a_log': dn_a_log,
            'dn_dt_bias': dn_dt_bias, 'dn_norm_w': dn_norm_w, 'w_dn_out': w_dn_out,
            's5_lam_re': s5_lam_re, 's5_lam_im': s5_lam_im, 's5_log_dt': s5_log_dt,
            's5_b_re': s5_b_re, 's5_b_im': s5_b_im, 's5_c_re': s5_c_re, 's5_c_im': s5_c_im,
            's5_d': s5_d, 'w_glu_a': w_glu_a, 'w_glu_b': w_glu_b, 'w_mix_out': w_mix_out,
            'ln1_g': ln1_g, 'ln1_b': ln1_b, 'w_router': w_router, 'b_router': b_router,
            'w_gate': w_gate, 'b_gate': b_gate, 'w_up': w_up, 'b_up': b_up,
            'w_down': w_down, 'b_down': b_down, 'ln2_g': ln2_g, 'ln2_b': ln2_b}


def reference(x, w_in, dn_conv_w, dn_a_log, dn_dt_bias, dn_norm_w, w_dn_out,
              s5_lam_re, s5_lam_im, s5_log_dt, s5_b_re, s5_b_im, s5_c_re, s5_c_im, s5_d,
              w_glu_a, w_glu_b, w_mix_out, ln1_g, ln1_b, w_router, b_router,
              w_gate, b_gate, w_up, b_up, w_down, b_down, ln2_g, ln2_b):
    h = x
    for l in range(DEPTH):
        mix = token_mixer(h, w_in[l], dn_conv_w[l], dn_a_log[l], dn_dt_bias[l], dn_norm_w[l], w_dn_out[l],
                          s5_lam_re[l], s5_lam_im[l], s5_log_dt[l], s5_b_re[l], s5_b_im[l],
                          s5_c_re[l], s5_c_im[l], s5_d[l], w_glu_a[l], w_glu_b[l], w_mix_out[l])
        h = layer_norm(DEEPNORM_ALPHA * h + mix, ln1_g[l], ln1_b[l])
        ffn = routed_experts(h, w_router[l], b_router[l], w_gate[l], b_gate[l], w_up[l], b_up[l],
                             w_down[l], b_down[l])
        h = layer_norm(DEEPNORM_ALPHA * h + ffn, ln2_g[l], ln2_b[l])
    return h
```

```python
import functools
import math

import jax
import jax.numpy as jnp
from jax import lax
from jax.experimental import pallas as pl
from jax.experimental.pallas import tpu as pltpu

F32 = jnp.float32
BF16 = jnp.bfloat16

D_MODEL = 2048
DEPTH = 4
S5_P = 16
S5_N = 64
S5_G = D_MODEL // S5_P
S5_L = 16
DN_QK_HEADS = 16
DN_V_HEADS = 32
DN_HD = 128
DN_KEY_W = DN_QK_HEADS * DN_HD
DN_VAL_W = DN_V_HEADS * DN_HD
DN_CONV_CH = 2 * DN_KEY_W + DN_VAL_W
DN_CONV = 4
DN_C = 128
N_EXPERTS = 32
TOP_K = 4
D_EXPERT = 512
SWIGLU_LIMIT = 7.0
SWIGLU_ALPHA = 1.702
MOE_BM = 256
ALPHA = (2 * DEPTH) ** 0.25
LN_EPS = 1e-5
RMS_EPS = 1e-6
L2_EPS = 1e-6
S5_MAX_REAL = -1e-4
OFF_QKV = D_MODEL
OFF_Z = OFF_QKV + DN_CONV_CH
OFF_BETA = OFF_Z + DN_VAL_W
OFF_GATE_S5 = OFF_BETA + 2 * DN_V_HEADS
IN_COLS = OFF_GATE_S5 + 2 * D_MODEL
LANE = 128
VMEM_LIMIT = 56 * 1024 * 1024


def _cparams(sem):
    return pltpu.CompilerParams(dimension_semantics=sem, vmem_limit_bytes=VMEM_LIMIT)


def _mm_kernel(*refs, nw, nx, epilogue):
    a_ref = refs[0]
    w_refs = refs[1:1 + nw]
    x_refs = refs[1 + nw:1 + nw + nx]
    o_ref = refs[1 + nw + nx]
    wbf_refs = refs[2 + nw + nx:]

    @pl.when(pl.program_id(1) == 0)
    def _():
        for w_ref, wbf_ref in zip(w_refs, wbf_refs):
            wbf_ref[...] = w_ref[...].astype(BF16)

    a = a_ref[...]
    accs = [jnp.dot(a, wbf[...], preferred_element_type=F32) for wbf in wbf_refs]
    o_ref[...] = epilogue(*accs, *[x[...] for x in x_refs]).astype(o_ref.dtype)


def _matmul(a, ws, n_out, out_dtype, epilogue, extras=(), tm=512, tn=512):
    m, k = a.shape
    tm = min(tm, m)
    grid = (n_out // tn, m // tm)
    in_specs = [pl.BlockSpec((tm, k), lambda j, i: (i, 0))]
    args = [a]
    for w, lead, off in ws:
        nlead = len(lead)
        in_specs.append(pl.BlockSpec((None,) * nlead + (k, tn),
                                     lambda j, i, lead=lead, off=off: tuple(lead) + (0, j + off)))
        args.append(w)
    for x, off in extras:
        in_specs.append(pl.BlockSpec((tm, tn), lambda j, i, off=off: (i, j + off)))
        args.append(x)
    return pl.pallas_call(
        functools.partial(_mm_kernel, nw=len(ws), nx=len(extras), epilogue=epilogue),
        grid=grid,
        in_specs=in_specs,
        out_specs=pl.BlockSpec((tm, tn), lambda j, i: (i, j)),
        out_shape=jax.ShapeDtypeStruct((m, n_out), out_dtype),
        scratch_shapes=[pltpu.VMEM((k, tn), BF16) for _ in ws],
        compiler_params=_cparams(("arbitrary", "arbitrary")),
    )(*args)


def _sigmoid(x):
    return 1.0 / (1.0 + jnp.exp(-x))


def _ln_kernel(h_ref, r_ref, g_ref, b_ref, o_ref, obf_ref):
    x = ALPHA * h_ref[...] + r_ref[...]
    mu = jnp.mean(x, axis=-1, keepdims=True)
    xc = x - mu
    var = jnp.mean(xc * xc, axis=-1, keepdims=True)
    y = xc * lax.rsqrt(var + LN_EPS) * g_ref[...] + b_ref[...]
    o_ref[...] = y
    obf_ref[...] = y.astype(BF16)


def _deepnorm(h, r, g, b, tm=256):
    t, d = h.shape
    tm = min(tm, t)
    row = pl.BlockSpec((tm, d), lambda i: (i, 0))
    vec = pl.BlockSpec((1, d), lambda i: (0, 0))
    return pl.pallas_call(
        _ln_kernel,
        grid=(t // tm,),
        in_specs=[row, row, vec, vec],
        out_specs=[row, row],
        out_shape=[jax.ShapeDtypeStruct((t, d), F32), jax.ShapeDtypeStruct((t, d), BF16)],
        compiler_params=_cparams(("arbitrary",)),
    )(h, r, g.reshape(1, d), b.reshape(1, d))


def _s5_operators(lam_re, lam_im, log_dt, b_re, b_im, c_re, c_im, d_skip, n_chunks):
    hi = lax.Precision.HIGHEST
    L, P, N = S5_L, S5_P, S5_N
    g = lam_re.shape[0]
    lre = jnp.minimum(lam_re, S5_MAX_REAL)
    lim = lam_im
    dt = jnp.exp(log_dt)[:, None]
    ks = jnp.arange(L + 1, dtype=F32)[:, None, None]
    mag = jnp.exp(lre * dt * ks)
    pr = mag * jnp.cos(lim * dt * ks)
    pi = mag * jnp.sin(lim * dt * ks)
    a_re, a_im = pr[1], pi[1]
    den = lre * lre + lim * lim
    f_re = ((a_re - 1.0) * lre + a_im * lim) / den
    f_im = (a_im * lre - (a_re - 1.0) * lim) / den
    bb_re = f_re[..., None] * b_re - f_im[..., None] * b_im
    bb_im = f_re[..., None] * b_im + f_im[..., None] * b_re
    qr = pr[L - 1::-1][:L]
    qi = pi[L - 1::-1][:L]
    bbr = jnp.transpose(bb_re, (0, 2, 1))[None]
    bbi = jnp.transpose(bb_im, (0, 2, 1))[None]
    bm_re = qr[:, :, None, :] * bbr - qi[:, :, None, :] * bbi
    bm_im = qr[:, :, None, :] * bbi + qi[:, :, None, :] * bbr
    bmat = jnp.concatenate([bm_re, bm_im], axis=-1)
    bmat = jnp.transpose(bmat, (1, 0, 2, 3)).reshape(g, L * P, 2 * N)
    ur = pr[1:, :, None, :]
    ui = pi[1:, :, None, :]
    ca_re = c_re[None] * ur - c_im[None] * ui
    ca_im = c_re[None] * ui + c_im[None] * ur
    cmat = jnp.concatenate([jnp.transpose(ca_re, (1, 3, 0, 2)),
                            -jnp.transpose(ca_im, (1, 3, 0, 2))], axis=1)
    cmat = cmat.reshape(g, 2 * N, L * P)
    vr = pr[:L, :, None, :]
    vi = pi[:L, :, None, :]
    cd_re = c_re[None] * vr - c_im[None] * vi
    cd_im = c_re[None] * vi + c_im[None] * vr
    kk = (jnp.einsum('dgpn,gnq->dgpq', cd_re, bb_re, precision=hi)
          - jnp.einsum('dgpn,gnq->dgpq', cd_im, bb_im, precision=hi))
    lag = jnp.arange(L)[None, :] - jnp.arange(L)[:, None]
    kx = kk[jnp.clip(lag, 0, L - 1)]
    kx = jnp.where((lag >= 0)[:, :, None, None, None], kx, 0.0)
    tmat = jnp.transpose(kx, (2, 0, 4, 1, 3)).reshape(g, L * P, L * P)
    nlev = max(1, int(math.log2(n_chunks)))
    mr, mi = pr[L], pi[L]
    m1, m2 = [], []
    for _ in range(nlev):
        m1.append(jnp.concatenate([mr, mr], axis=-1))
        m2.append(jnp.concatenate([-mi, mi], axis=-1))
        mr, mi = mr * mr - mi * mi, 2.0 * mr * mi
    pad = [jnp.zeros_like(m1[0])] * (16 - nlev)
    pw1 = jnp.stack(m1 + pad, axis=1)
    pw2 = jnp.stack(m2 + pad, axis=1)
    dsk = jnp.tile(d_skip, (1, L)).reshape(g, 1, L * P)
    return tmat.astype(BF16), bmat.astype(BF16), cmat.astype(BF16), pw1, pw2, dsk


def _gelu_tanh(y):
    return 0.5 * y * (1.0 + jnp.tanh(0.7978845608028654 * (y + 0.044715 * y * y * y)))


def _s5_kernel(u_ref, t_ref, b_ref, c_ref, pw1_ref, pw2_ref, d_ref, o_ref, *, gb, nlev):
    nc = u_ref.shape[1]
    row = lax.broadcasted_iota(jnp.int32, (nc, 2 * S5_N), 0)
    for gi in range(gb):
        u = u_ref[gi]
        x = jnp.dot(u, b_ref[gi], preferred_element_type=F32)
        for lev in range(nlev):
            d = 1 << lev
            m1 = pw1_ref[gi, lev:lev + 1, :]
            m2 = pw2_ref[gi, lev:lev + 1, :]
            sh = jnp.where(row >= d, pltpu.roll(x, d, axis=0), 0.0)
            x = x + m1 * sh + m2 * pltpu.roll(sh, S5_N, axis=1)
        sprev = jnp.where(row >= 1, pltpu.roll(x, 1, axis=0), 0.0)
        y = (jnp.dot(u, t_ref[gi], preferred_element_type=F32)
             + jnp.dot(sprev.astype(BF16), c_ref[gi], preferred_element_type=F32)
             + d_ref[gi] * u.astype(F32))
        o_ref[gi] = _gelu_tanh(y).astype(o_ref.dtype)


def _s5_apply(u_fold, ops, gb=4):
    tmat, bmat, cmat, pw1, pw2, dsk = ops
    g, nc, lp = u_fold.shape
    nlev = int(math.log2(nc))
    assert (1 << nlev) == nc

    def spec(shape):
        return pl.BlockSpec((gb,) + shape, lambda i: (i,) + (0,) * len(shape))

    return pl.pallas_call(
        functools.partial(_s5_kernel, gb=gb, nlev=nlev),
        grid=(g // gb,),
        in_specs=[spec((nc, lp)), spec((lp, lp)), spec((lp, 2 * S5_N)), spec((2 * S5_N, lp)),
                  spec((16, 2 * S5_N)), spec((16, 2 * S5_N)), spec((1, lp))],
        out_specs=spec((nc, lp)),
        out_shape=jax.ShapeDtypeStruct((g, nc, lp), BF16),
        compiler_params=_cparams(("arbitrary",)),
    )(u_fold, tmat, bmat, cmat, pw1, pw2, dsk)


def _conv_kernel(x_ref, p_ref, w_ref, o_ref, *, nq, nqk):
    j = pl.program_id(0)
    i = pl.program_id(1)
    x = x_ref[...].astype(F32)
    tm, tc = x.shape
    prev = jnp.where(i > 0, p_ref[...].astype(F32), 0.0)
    w = w_ref[...]
    row8 = lax.broadcasted_iota(jnp.int32, (8, tc), 0)
    acc = x * w[DN_CONV - 1:DN_CONV, :]
    for s in range(1, DN_CONV):
        r = pltpu.roll(x, s, axis=0)
        pr = pltpu.roll(prev, s, axis=0)
        top = jnp.where(row8 < s, pr[:8], r[:8])
        xs = jnp.concatenate([top, r[8:]], axis=0)
        acc = acc + xs * w[DN_CONV - 1 - s:DN_CONV - s, :]
    y = acc * _sigmoid(acc)
    qscale = jnp.where(j < nq, DN_HD ** -0.5, 1.0)
    for hh in range(tc // DN_HD):
        blk = y[:, hh * DN_HD:(hh + 1) * DN_HD]
        ss = jnp.sum(blk * blk, axis=-1, keepdims=True)
        fac = jnp.where(j < nqk, lax.rsqrt(ss + L2_EPS) * qscale, 1.0)
        o_ref[:, hh * DN_HD:(hh + 1) * DN_HD] = (blk * fac).astype(o_ref.dtype)


def _dn_conv(proj, conv_wt, tm=512, tc=512):
    t = proj.shape[0]
    tm = min(tm, t)
    off = OFF_QKV // tc
    pb = tm // 16
    return pl.pallas_call(
        functools.partial(_conv_kernel, nq=DN_KEY_W // tc, nqk=2 * DN_KEY_W // tc),
        grid=(DN_CONV_CH // tc, t // tm),
        in_specs=[pl.BlockSpec((tm, tc), lambda j, i: (i, j + off)),
                  pl.BlockSpec((16, tc), lambda j, i: (jnp.maximum(i * pb - 1, 0), j + off)),
                  pl.BlockSpec((DN_CONV, tc), lambda j, i: (0, j))],
        out_specs=pl.BlockSpec((tm, tc), lambda j, i: (i, j)),
        out_shape=jax.ShapeDtypeStruct((t, DN_CONV_CH), BF16),
        compiler_params=_cparams(("arbitrary", "arbitrary")),
    )(proj, proj, conv_wt)


def _dn_gate_kernel(x_ref, a_ref, dtb_ref, o_ref):
    x = x_ref[...]
    tm, w = x.shape
    lane = lax.broadcasted_iota(jnp.int32, (tm, w), 1)
    row = lax.broadcasted_iota(jnp.int32, (tm, w), 0)
    xs = x + dtb_ref[...]
    softplus = jnp.maximum(xs, 0.0) + jnp.log(1.0 + jnp.exp(-jnp.abs(xs)))
    g = -jnp.exp(a_ref[...]) * softplus
    pos = row & (DN_C - 1)
    d = 1
    while d < DN_C:
        g = g + jnp.where(pos >= d, pltpu.roll(g, d, axis=0), 0.0)
        d *= 2
    o_ref[...] = jnp.where(lane < DN_V_HEADS, _sigmoid(x), g)


def _dn_gates(bd, a_log, dt_bias, tm=512):
    t = bd.shape[0]
    tm = min(tm, t)
    zeros = jnp.zeros((DN_V_HEADS,), F32)
    a2 = jnp.concatenate([zeros, a_log]).reshape(1, -1)
    b2 = jnp.concatenate([zeros, dt_bias]).reshape(1, -1)
    w = 2 * DN_V_HEADS
    return pl.pallas_call(
        _dn_gate_kernel,
        grid=(t // tm,),
        in_specs=[pl.BlockSpec((tm, w), lambda i: (i, 0)),
                  pl.BlockSpec((1, w), lambda i: (0, 0)),
                  pl.BlockSpec((1, w), lambda i: (0, 0))],
        out_specs=pl.BlockSpec((tm, w), lambda i: (i, 0)),
        out_shape=jax.ShapeDtypeStruct((t, w), F32),
        compiler_params=_cparams(("arbitrary",)),
    )(bd, a2, b2)


def _split_bf16(x):
    hi = x.astype(BF16)
    lo = (x - hi.astype(F32)).astype(BF16)
    return hi, lo


def _dot3(a, b):
    ah, al = _split_bf16(a)
    bh, bl = _split_bf16(b)
    return (jnp.dot(ah, bh, preferred_element_type=F32)
            + jnp.dot(ah, bl, preferred_element_type=F32)
            + jnp.dot(al, bh, preferred_element_type=F32))


def _dot1(a, b):
    return jnp.dot(a.astype(BF16), b.astype(BF16), preferred_element_type=F32)


def _unit_lower_inverse(a, ii, jj):
    c = a.shape[0]
    eye = (ii == jj).astype(F32)
    p = jnp.where((ii >> 3) == (jj >> 3), -a, 0.0)
    t = eye + p
    p2 = _dot3(p, p)
    t = t + _dot3(t, p2)
    p4 = _dot3(p2, p2)
    t = t + _dot3(t, p4)
    s = 8
    sh = 3
    while s < c:
        bi = ii >> sh
        bj = jj >> sh
        off = jnp.where(((bi & 1) == 1) & (bj == bi - 1), a, 0.0)
        t = t - _dot1(_dot1(t, off), t)
        s *= 2
        sh += 1
    return t


def _delta_kernel(q_ref, k_ref, v_ref, z_ref, bg_ref, gr_ref, nw_ref, o_ref, s_ref, *, nchunk):
    hh = pl.program_id(0)
    i = pl.program_id(1)

    @pl.when(i == 0)
    def _():
        s_ref[...] = jnp.zeros_like(s_ref)

    c = DN_C
    bg = bg_ref[...]
    lane = lax.broadcasted_iota(jnp.int32, bg.shape, 1)
    beta_col = jnp.sum(jnp.where(lane == hh, bg, 0.0), axis=1, keepdims=True)
    gc_col = jnp.sum(jnp.where(lane == hh + DN_V_HEADS, bg, 0.0), axis=1, keepdims=True)
    ii = lax.broadcasted_iota(jnp.int32, (c, c), 0)
    jj = lax.broadcasted_iota(jnp.int32, (c, c), 1)
    nw = nw_ref[...]
    s = s_ref[...]
    for ci in range(nchunk):
        r0 = ci * c
        q = q_ref[r0:r0 + c, :]
        k = k_ref[r0:r0 + c, :]
        kf = k.astype(F32)
        v = v_ref[r0:r0 + c, :].astype(F32)
        b = beta_col[r0:r0 + c, :]
        gc = gc_col[r0:r0 + c, :]
        gr = gr_ref[ci]
        decay = jnp.exp(jnp.where(ii >= jj, gc - gr, -jnp.inf))
        kk = lax.dot_general(k, k, (((1,), (1,)), ((), ())), preferred_element_type=F32)
        qk = lax.dot_general(q, k, (((1,), (1,)), ((), ())), preferred_element_type=F32)
        a = jnp.where(ii > jj, kk * decay, 0.0) * b
        attn = qk * decay
        tinv = _unit_lower_inverse(a, ii, jj)
        eg = jnp.exp(gc)
        rhs = jnp.concatenate([v * b, kf * (b * eg)], axis=1)
        sol = _dot1(tinv, rhs)
        u_c = sol[:, :DN_HD]
        w_c = sol[:, DN_HD:]
        sb = s.astype(BF16)
        v_new = u_c - jnp.dot(w_c.astype(BF16), sb, preferred_element_type=F32)
        qd = (q.astype(F32) * eg).astype(BF16)
        o = (jnp.dot(qd, sb, preferred_element_type=F32)
             + _dot1(attn, v_new))
        g_last = gc[c - 1:c, :]
        kd = kf * jnp.exp(g_last - gc)
        s = s * jnp.exp(g_last) + _dot1(kd.T, v_new)
        ms = jnp.mean(o * o, axis=-1, keepdims=True)
        z = z_ref[r0:r0 + c, :].astype(F32)
        o = o * lax.rsqrt(ms + RMS_EPS) * nw * (z * _sigmoid(z))
        o_ref[r0:r0 + c, :] = o.astype(o_ref.dtype)
    s_ref[...] = s


def _delta_rule(qkv, proj, bg, gc_rows, norm_w, rb=512):
    t = qkv.shape[0]
    rb = min(rb, t)
    nchunk = rb // DN_C
    kq = DN_KEY_W // DN_HD
    zoff = OFF_Z // DN_HD
    rep = DN_V_HEADS // DN_QK_HEADS
    return pl.pallas_call(
        functools.partial(_delta_kernel, nchunk=nchunk),
        grid=(DN_V_HEADS, t // rb),
        in_specs=[pl.BlockSpec((rb, DN_HD), lambda h, i: (i, h // rep)),
                  pl.BlockSpec((rb, DN_HD), lambda h, i: (i, kq + h // rep)),
                  pl.BlockSpec((rb, DN_HD), lambda h, i: (i, 2 * kq + h)),
                  pl.BlockSpec((rb, DN_HD), lambda h, i: (i, zoff + h)),
                  pl.BlockSpec((rb, 2 * DN_V_HEADS), lambda h, i: (i, 0)),
                  pl.BlockSpec((None, nchunk, 1, DN_C), lambda h, i: (h, i, 0, 0)),
                  pl.BlockSpec((1, DN_HD), lambda h, i: (0, 0))],
        out_specs=pl.BlockSpec((rb, DN_HD), lambda h, i: (i, h)),
        out_shape=jax.ShapeDtypeStruct((t, DN_VAL_W), BF16),
        scratch_shapes=[pltpu.VMEM((DN_HD, DN_HD), F32)],
        compiler_params=_cparams(("arbitrary", "arbitrary")),
    )(qkv, qkv, qkv, proj, bg, gc_rows, norm_w.reshape(1, DN_HD))


def _router_kernel(h_ref, w_ref, b_ref, idx_ref, wt_ref):
    logits = _dot3(h_ref[...], w_ref[...]) + b_ref[...]
    tm, e = logits.shape
    lane = lax.broadcasted_iota(jnp.int32, (tm, e), 1)
    lane_o = lax.broadcasted_iota(jnp.int32, (tm, LANE), 1)
    idx_out = jnp.zeros((tm, LANE), jnp.int32)
    val_out = jnp.zeros((tm, LANE), F32)
    cur = logits
    vals = []
    for kth in range(TOP_K):
        m = jnp.max(cur, axis=-1, keepdims=True)
        sel = jnp.min(jnp.where(cur == m, lane, e), axis=-1, keepdims=True)
        cur = jnp.where(lane == sel, -jnp.inf, cur)
        idx_out = jnp.where(lane_o == kth, sel, idx_out)
        vals.append(m)
    es = [jnp.exp(v - vals[0]) for v in vals]
    tot = es[0]
    for x in es[1:]:
        tot = tot + x
    for kth in range(TOP_K):
        val_out = jnp.where(lane_o == kth, es[kth] / tot, val_out)
    idx_ref[...] = idx_out
    wt_ref[...] = val_out


def _router(h, w_router, b_router, tm=512):
    t, d = h.shape
    tm = min(tm, t)
    e = w_router.shape[1]
    idx, wt = pl.pallas_call(
        _router_kernel,
        grid=(t // tm,),
        in_specs=[pl.BlockSpec((tm, d), lambda i: (i, 0)),
                  pl.BlockSpec((d, e), lambda i: (0, 0)),
                  pl.BlockSpec((1, e), lambda i: (0, 0))],
        out_specs=[pl.BlockSpec((tm, LANE), lambda i: (i, 0)),
                   pl.BlockSpec((tm, LANE), lambda i: (i, 0))],
        out_shape=[jax.ShapeDtypeStruct((t, LANE), jnp.int32),
                   jax.ShapeDtypeStruct((t, LANE), F32)],
        compiler_params=_cparams(("arbitrary",)),
    )(h, w_router, b_router.reshape(1, e))
    return idx[:, :TOP_K], wt[:, :TOP_K]


def _expert_kernel(be_ref, na_ref, x_ref, wg_ref, wu_ref, wd_ref, bg_ref, bu_ref, bd_ref,
                   o_ref, wgb_ref, wub_ref, wdb_ref):
    i = pl.program_id(0)
    prev = be_ref[jnp.maximum(i - 1, 0)]
    changed = jnp.logical_or(i == 0, be_ref[i] != prev)

    @pl.when(changed)
    def _():
        wgb_ref[...] = wg_ref[...].astype(BF16)
        wub_ref[...] = wu_ref[...].astype(BF16)
        wdb_ref[...] = wd_ref[...].astype(BF16)

    @pl.when(i < na_ref[0])
    def _():
        x = x_ref[...]
        gate = jnp.minimum(jnp.dot(x, wgb_ref[...], preferred_element_type=F32) + bg_ref[...],
                           SWIGLU_LIMIT)
        up = jnp.clip(jnp.dot(x, wub_ref[...], preferred_element_type=F32) + bu_ref[...],
                      -SWIGLU_LIMIT, SWIGLU_LIMIT)
        hid = gate * _sigmoid(SWIGLU_ALPHA * gate) * (up + 1.0)
        y = jnp.dot(hid.astype(BF16), wdb_ref[...], preferred_element_type=F32) + bd_ref[...]
        o_ref[...] = y.astype(o_ref.dtype)

    @pl.when(i >= na_ref[0])
    def _():
        o_ref[...] = jnp.zeros_like(o_ref)


def _experts(xs, block_e, n_active, layer, w_gate, b_gate, w_up, b_up, w_down, b_down):
    n_slots, d = xs.shape
    nb = n_slots // MOE_BM
    f = w_gate.shape[-1]
    l = layer

    def wspec(shape):
        return pl.BlockSpec((None, None) + shape, lambda i, be, na: (l, be[i], 0, 0))

    grid_spec = pltpu.PrefetchScalarGridSpec(
        num_scalar_prefetch=2,
        grid=(nb,),
        in_specs=[pl.BlockSpec((MOE_BM, d), lambda i, be, na: (i, 0)),
                  wspec((d, f)), wspec((d, f)), wspec((f, d)),
                  wspec((1, f)), wspec((1, f)), wspec((1, d))],
        out_specs=pl.BlockSpec((MOE_BM, d), lambda i, be, na: (i, 0)),
        scratch_shapes=[pltpu.VMEM((d, f), BF16), pltpu.VMEM((d, f), BF16), pltpu.VMEM((f, d), BF16)],
    )
    nl, ne = b_gate.shape[:2]
    return pl.pallas_call(
        _expert_kernel,
        grid_spec=grid_spec,
        out_shape=jax.ShapeDtypeStruct((n_slots, d), BF16),
        compiler_params=_cparams(("arbitrary",)),
    )(block_e, n_active, xs, w_gate, w_up, w_down,
      b_gate.reshape(nl, ne, 1, f), b_up.reshape(nl, ne, 1, f), b_down.reshape(nl, ne, 1, d))


def _routing(top_idx):
    t = top_idx.shape[0]
    sel = jnp.sum(jax.nn.one_hot(top_idx, N_EXPERTS, dtype=jnp.int32), axis=1)
    counts = jnp.sum(sel, axis=0)
    before = jnp.cumsum(sel, axis=0) - sel
    padded = (counts + MOE_BM - 1) // MOE_BM * MOE_BM
    pend = jnp.cumsum(padded)
    pstart = pend - padded
    dest = pstart[top_idx] + jnp.take_along_axis(before, top_idx, axis=1)
    n_blocks = t * TOP_K // MOE_BM + N_EXPERTS
    n_active = (pend[-1] // MOE_BM).astype(jnp.int32)
    blk = jnp.arange(n_blocks, dtype=jnp.int32)
    blk = jnp.minimum(blk, jnp.maximum(n_active - 1, 0))
    block_e = jnp.sum((blk[:, None] * MOE_BM >= pend[None, :]).astype(jnp.int32), axis=1)
    block_e = jnp.minimum(block_e, N_EXPERTS - 1).astype(jnp.int32)
    tok = jnp.broadcast_to(jnp.arange(t, dtype=jnp.int32)[:, None], dest.shape)
    slot_tok = jnp.zeros((n_blocks * MOE_BM,), jnp.int32).at[dest.reshape(-1)].set(tok.reshape(-1))
    return dest, slot_tok, block_e, n_active.reshape(1)


def _layer(l, h, hb, p):
    t = h.shape[0]
    w_in = p['w_in']
    proj = _matmul(hb, [(w_in, (l,), 0)], OFF_BETA, BF16, lambda acc: acc)
    gates = _matmul(hb, [(w_in[l, :, OFF_GATE_S5:], (), 0)], 2 * D_MODEL, BF16,
                    lambda acc: _sigmoid(acc))
    bd = _matmul(hb, [(w_in[l, :, OFF_BETA:OFF_GATE_S5], (), 0)], 2 * DN_V_HEADS, F32,
                 lambda acc: acc, tn=2 * DN_V_HEADS)

    nc = t // S5_L
    ops = _s5_operators(p['s5_lam_re'][l], p['s5_lam_im'][l], p['s5_log_dt'][l], p['s5_b_re'][l],
                        p['s5_b_im'][l], p['s5_c_re'][l], p['s5_c_im'][l], p['s5_d'][l], nc)
    u_fold = proj[:, :D_MODEL].reshape(nc, S5_L, S5_G, S5_P).transpose(2, 0, 1, 3)
    y_fold = _s5_apply(u_fold.reshape(S5_G, nc, S5_L * S5_P), ops)
    y = y_fold.reshape(S5_G, nc, S5_L, S5_P).transpose(1, 2, 0, 3).reshape(t, D_MODEL)
    part = _matmul(y, [(p['w_glu_a'], (l,), 0), (p['w_glu_b'], (l,), 0)], D_MODEL, BF16,
                   lambda a, b, g: a * _sigmoid(b) * g.astype(F32), extras=[(gates, 0)])

    qkv = _dn_conv(proj, p['dn_conv_w'][l].T)
    bg = _dn_gates(bd, p['dn_a_log'][l], p['dn_dt_bias'][l])
    gc_rows = bg[:, DN_V_HEADS:].T.reshape(DN_V_HEADS, t // DN_C, 1, DN_C)
    o = _delta_rule(qkv, proj, bg, gc_rows, p['dn_norm_w'][l])
    merged = _matmul(o, [(p['w_dn_out'], (l,), 0)], D_MODEL, BF16,
                     lambda acc, g, s: acc * g.astype(F32) + s.astype(F32),
                     extras=[(gates, D_MODEL // 512), (part, 0)])
    mix = _matmul(merged, [(p['w_mix_out'], (l,), 0)], D_MODEL, F32, lambda acc: acc)
    h, hb = _deepnorm(h, mix, p['ln1_g'][l], p['ln1_b'][l])

    top_idx, top_w = _router(h, p['w_router'][l], p['b_router'][l])
    dest, slot_tok, block_e, n_active = _routing(top_idx)
    xs = jnp.take(hb, slot_tok, axis=0)
    ys = _experts(xs, block_e, n_active, l, p['w_gate'], p['b_gate'], p['w_up'], p['b_up'],
                  p['w_down'], p['b_down'])
    ffn = jnp.sum(jnp.take(ys, dest, axis=0).astype(F32) * top_w[..., None], axis=1)
    return _deepnorm(h, ffn, p['ln2_g'][l], p['ln2_b'][l])


def kernel(x, w_in, dn_conv_w, dn_a_log, dn_dt_bias, dn_norm_w, w_dn_out, s5_lam_re, s5_lam_im, s5_log_dt, s5_b_re, s5_b_im, s5_c_re, s5_c_im, s5_d, w_glu_a, w_glu_b, w_mix_out, ln1_g, ln1_b, w_router, b_router, w_gate, b_gate, w_up, b_up, w_down, b_down, ln2_g, ln2_b):
    p = dict(w_in=w_in, dn_conv_w=dn_conv_w, dn_a_log=dn_a_log, dn_dt_bias=dn_dt_bias,
             dn_norm_w=dn_norm_w, w_dn_out=w_dn_out, s5_lam_re=s5_lam_re, s5_lam_im=s5_lam_im,
             s5_log_dt=s5_log_dt, s5_b_re=s5_b_re, s5_b_im=s5_b_im, s5_c_re=s5_c_re,
             s5_c_im=s5_c_im, s5_d=s5_d, w_glu_a=w_glu_a, w_glu_b=w_glu_b, w_mix_out=w_mix_out,
             ln1_g=ln1_g, ln1_b=ln1_b, w_router=w_router, b_router=b_router, w_gate=w_gate,
             b_gate=b_gate, w_up=w_up, b_up=b_up, w_down=w_down, b_down=b_down,
             ln2_g=ln2_g, ln2_b=ln2_b)
    bsz, t, d = x.shape
    h = x.reshape(bsz * t, d)
    hb = h.astype(BF16)
    for l in range(w_in.shape[0]):
        h, hb = _layer(l, h, hb, p)
    return h.reshape(bsz, t, d)
```

```python
import functools
import math

import jax
import jax.numpy as jnp
from jax import lax
from jax.experimental import pallas as pl
from jax.experimental.pallas import tpu as pltpu

F32 = jnp.float32
BF16 = jnp.bfloat16

D_MODEL = 2048
DEPTH = 4
S5_P = 16
S5_N = 64
S5_G = D_MODEL // S5_P
S5_L = 16
DN_QK_HEADS = 16
DN_V_HEADS = 32
DN_HD = 128
DN_KEY_W = DN_QK_HEADS * DN_HD
DN_VAL_W = DN_V_HEADS * DN_HD
DN_CONV_CH = 2 * DN_KEY_W + DN_VAL_W
DN_CONV = 4
DN_C = 128
N_EXPERTS = 32
TOP_K = 4
D_EXPERT = 512
SWIGLU_LIMIT = 7.0
SWIGLU_ALPHA = 1.702
MOE_BM = 256
ALPHA = (2 * DEPTH) ** 0.25
LN_EPS = 1e-5
RMS_EPS = 1e-6
L2_EPS = 1e-6
S5_MAX_REAL = -1e-4
OFF_QKV = D_MODEL
OFF_Z = OFF_QKV + DN_CONV_CH
OFF_BETA = OFF_Z + DN_VAL_W
OFF_GATE_S5 = OFF_BETA + 2 * DN_V_HEADS
IN_COLS = OFF_GATE_S5 + 2 * D_MODEL
LANE = 128
VMEM_LIMIT = 56 * 1024 * 1024


def _cparams(sem):
    return pltpu.CompilerParams(dimension_semantics=sem, vmem_limit_bytes=VMEM_LIMIT)


def _mm_kernel(*refs, nw, nx, epilogue):
    a_ref = refs[0]
    w_refs = refs[1:1 + nw]
    x_refs = refs[1 + nw:1 + nw + nx]
    o_ref = refs[1 + nw + nx]
    wbf_refs = refs[2 + nw + nx:]

    @pl.when(pl.program_id(1) == 0)
    def _():
        for w_ref, wbf_ref in zip(w_refs, wbf_refs):
            wbf_ref[...] = w_ref[...].astype(BF16)

    a = a_ref[...].astype(BF16)
    accs = [jnp.dot(a, wbf[...], preferred_element_type=F32) for wbf in wbf_refs]
    o_ref[...] = epilogue(*accs, *[x[...] for x in x_refs]).astype(o_ref.dtype)


def _matmul(a, ws, n_out, out_dtype, epilogue, extras=(), tm=512, tn=512):
    m, k = a.shape
    tm = min(tm, m)
    grid = (n_out // tn, m // tm)
    in_specs = [pl.BlockSpec((tm, k), lambda j, i: (i, 0))]
    args = [a]
    for w, lead, off in ws:
        nlead = len(lead)
        in_specs.append(pl.BlockSpec((None,) * nlead + (k, tn),
                                     lambda j, i, lead=lead, off=off: tuple(lead) + (0, j + off)))
        args.append(w)
    for x, off in extras:
        in_specs.append(pl.BlockSpec((tm, tn), lambda j, i, off=off: (i, j + off)))
        args.append(x)
    return pl.pallas_call(
        functools.partial(_mm_kernel, nw=len(ws), nx=len(extras), epilogue=epilogue),
        grid=grid,
        in_specs=in_specs,
        out_specs=pl.BlockSpec((tm, tn), lambda j, i: (i, j)),
        out_shape=jax.ShapeDtypeStruct((m, n_out), out_dtype),
        scratch_shapes=[pltpu.VMEM((k, tn), BF16) for _ in ws],
        compiler_params=_cparams(("arbitrary", "arbitrary")),
    )(*args)


def _sigmoid(x):
    return 1.0 / (1.0 + jnp.exp(-x))


def _split_bf16(x):
    hi = x.astype(BF16)
    lo = (x - hi.astype(F32)).astype(BF16)
    return hi, lo


def _dot3(a, b, dims=(((1,), (0,)), ((), ()))):
    ah, al = _split_bf16(a)
    bh, bl = _split_bf16(b)
    return (lax.dot_general(ah, bh, dims, preferred_element_type=F32)
            + lax.dot_general(ah, bl, dims, preferred_element_type=F32)
            + lax.dot_general(al, bh, dims, preferred_element_type=F32))


_NT = (((1,), (1,)), ((), ()))


def _layer_norm_rows(x, g, b):
    mu = jnp.mean(x, axis=-1, keepdims=True)
    xc = x - mu
    var = jnp.mean(xc * xc, axis=-1, keepdims=True)
    return xc * lax.rsqrt(var + LN_EPS) * g + b


def _ln_kernel(h_ref, r_ref, g_ref, b_ref, o_ref, obf_ref):
    y = _layer_norm_rows(ALPHA * h_ref[...] + r_ref[...], g_ref[...], b_ref[...])
    o_ref[...] = y
    obf_ref[...] = y.astype(BF16)


def _deepnorm(h, r, g, b, tm=256):
    t, d = h.shape
    tm = min(tm, t)
    row = pl.BlockSpec((tm, d), lambda i: (i, 0))
    vec = pl.BlockSpec((1, d), lambda i: (0, 0))
    return pl.pallas_call(
        _ln_kernel,
        grid=(t // tm,),
        in_specs=[row, row, vec, vec],
        out_specs=[row, row],
        out_shape=[jax.ShapeDtypeStruct((t, d), F32), jax.ShapeDtypeStruct((t, d), BF16)],
        compiler_params=_cparams(("arbitrary",)),
    )(h, r, g.reshape(1, d), b.reshape(1, d))


def _s5_tables(lam_re, lam_im, log_dt, b_re, b_im, c_re, c_im, d_skip, n_chunks):
    L = S5_L
    g = lam_re.shape[0]
    lre = jnp.minimum(lam_re, S5_MAX_REAL)
    lim = lam_im
    dt = jnp.exp(log_dt)[:, None]
    ks = jnp.arange(L + 1, dtype=F32)[:, None, None]
    mag = jnp.exp(lre * dt * ks)
    pr = mag * jnp.cos(lim * dt * ks)
    pi = mag * jnp.sin(lim * dt * ks)
    a_re, a_im = pr[1], pi[1]
    den = lre * lre + lim * lim
    f_re = ((a_re - 1.0) * lre + a_im * lim) / den
    f_im = (a_im * lre - (a_re - 1.0) * lim) / den
    bb_re = f_re[..., None] * b_re - f_im[..., None] * b_im
    bb_im = f_re[..., None] * b_im + f_im[..., None] * b_re
    bb = jnp.concatenate([jnp.transpose(bb_re, (0, 2, 1)), jnp.transpose(bb_im, (0, 2, 1))], axis=-1)
    cc = jnp.concatenate([c_re, c_im], axis=-1)
    qr = jnp.transpose(pr[L - 1::-1], (1, 0, 2))
    qi = jnp.transpose(pi[L - 1::-1], (1, 0, 2))
    p1 = jnp.concatenate([qr, qr], axis=-1)
    p2 = jnp.concatenate([-qi, qi], axis=-1)
    ur = jnp.transpose(pr[1:], (1, 0, 2))
    ui = jnp.transpose(pi[1:], (1, 0, 2))
    q1 = jnp.concatenate([ur, -ur], axis=-1)
    q2 = jnp.concatenate([-ui, -ui], axis=-1)
    nlev = max(1, int(math.log2(n_chunks)))
    mr, mi = pr[L], pi[L]
    m1, m2 = [], []
    for _ in range(nlev):
        m1.append(jnp.concatenate([mr, mr], axis=-1))
        m2.append(jnp.concatenate([-mi, mi], axis=-1))
        mr, mi = mr * mr - mi * mi, 2.0 * mr * mi
    pad = [jnp.zeros_like(m1[0])] * (16 - nlev)
    pw1 = jnp.stack(m1 + pad, axis=1)
    pw2 = jnp.stack(m2 + pad, axis=1)
    dsk = jnp.tile(d_skip, (1, L)).reshape(g, 1, L * S5_P)
    return p1, p2, q1, q2, bb, cc, pw1, pw2, dsk


def _gelu_tanh(y):
    return 0.5 * y * (1.0 + jnp.tanh(0.7978845608028654 * (y + 0.044715 * y * y * y)))


def _rep_rows(x, n):
    r, w = x.shape
    return jnp.broadcast_to(x[:, None, :], (r, n, w)).reshape(r * n, w)


def _tile_rows(x, n):
    r, w = x.shape
    return jnp.broadcast_to(x[None, :, :], (n, r, w)).reshape(n * r, w)


def _s5_kernel(u_ref, p1_ref, p2_ref, q1_ref, q2_ref, bb_ref, cc_ref, pw1_ref, pw2_ref, d_ref,
               o_ref, perm_ref, *, nlev, gb):
    L, P, N = S5_L, S5_P, S5_N
    lp = L * P
    half = (L // 2) * LANE
    nc = u_ref.shape[0] // L

    @pl.when(pl.program_id(0) == 0)
    def _():
        r = lax.broadcasted_iota(jnp.int32, (half, half), 0)
        c = lax.broadcasted_iota(jnp.int32, (half, half), 1)
        dst = ((r >> 4) & 7) * LANE + (r >> 7) * P + (r & 15)
        perm_ref[...] = jnp.where(c == dst, 1.0, 0.0).astype(BF16)

    perm = perm_ref[...]
    v = []
    for th in range(2):
        xs = [u_ref[pl.ds(th * 8 + tl, nc, stride=L), :].astype(BF16) for tl in range(8)]
        v.append(jnp.dot(jnp.concatenate(xs, axis=1), perm, preferred_element_type=F32).astype(BF16))

    row = lax.broadcasted_iota(jnp.int32, (nc, 2 * N), 0)
    lane_n = lax.broadcasted_iota(jnp.int32, (1, 2 * N), 1)
    sign = jnp.where(lane_n < N, 1.0, -1.0)
    rblk = lax.broadcasted_iota(jnp.int32, (lp, lp), 0) >> 4
    cblk = lax.broadcasted_iota(jnp.int32, (lp, lp), 1) >> 4
    z = [[], []]
    for gi in range(gb):
        ug = jnp.concatenate([v[0][:, gi * LANE:(gi + 1) * LANE],
                              v[1][:, gi * LANE:(gi + 1) * LANE]], axis=1)
        bbg = bb_ref[gi]
        ccg = cc_ref[gi]
        bmat = (_rep_rows(p1_ref[gi], P) * _tile_rows(bbg, L)
                + _rep_rows(p2_ref[gi], P) * _tile_rows(pltpu.roll(bbg, N, axis=1), L))
        cmt = (_rep_rows(q1_ref[gi], P) * _tile_rows(ccg, L)
               + _rep_rows(q2_ref[gi], P) * _tile_rows(pltpu.roll(ccg, N, axis=1), L))
        w = _dot3(bmat, _tile_rows(ccg * sign, L), _NT)
        tmat = jnp.zeros((lp, lp), F32)
        for t in range(L):
            shift = (lp - (L - 1 - t) * P) % lp
            tmat = jnp.where((cblk == t) & (rblk <= t), pltpu.roll(w, shift, axis=0), tmat)
        x = jnp.dot(ug, bmat.astype(BF16), preferred_element_type=F32)
        for lev in range(nlev):
            d = 1 << lev
            m1 = pw1_ref[gi, lev:lev + 1, :]
            m2 = pw2_ref[gi, lev:lev + 1, :]
            sh = jnp.where(row >= d, pltpu.roll(x, d, axis=0), 0.0)
            x = x + m1 * sh + m2 * pltpu.roll(sh, N, axis=1)
        sprev = jnp.where(row >= 1, pltpu.roll(x, 1, axis=0), 0.0)
        y = (jnp.dot(ug, tmat.astype(BF16), preferred_element_type=F32)
             + lax.dot_general(sprev.astype(BF16), cmt.astype(BF16), _NT, preferred_element_type=F32)
             + d_ref[gi] * ug.astype(F32))
        yg = _gelu_tanh(y).astype(BF16)
        z[0].append(yg[:, :LANE])
        z[1].append(yg[:, LANE:])
    for th in range(2):
        yp = lax.dot_general(jnp.concatenate(z[th], axis=1), perm, _NT, preferred_element_type=F32)
        for tl in range(8):
            o_ref[pl.ds(th * 8 + tl, nc, stride=L), :] = yp[:, tl * LANE:(tl + 1) * LANE]


def _s5_apply(u, tables):
    t, d = u.shape
    gb = LANE // S5_P
    nc = t // S5_L
    nlev = int(math.log2(nc))
    assert (1 << nlev) == nc and nlev <= 16
    tab = pl.BlockSpec((gb, 16, 2 * S5_N), lambda i: (i, 0, 0))
    return pl.pallas_call(
        functools.partial(_s5_kernel, nlev=nlev, gb=gb),
        grid=(d // LANE,),
        in_specs=[pl.BlockSpec((t, LANE), lambda i: (0, i))] + [tab] * 8
                 + [pl.BlockSpec((gb, 1, S5_L * S5_P), lambda i: (i, 0, 0))],
        out_specs=pl.BlockSpec((t, LANE), lambda i: (0, i)),
        out_shape=jax.ShapeDtypeStruct((t, d), F32),
        scratch_shapes=[pltpu.VMEM((8 * LANE, 8 * LANE), BF16)],
        compiler_params=_cparams(("arbitrary",)),
    )(u, *tables)


def _conv_kernel(x_ref, p_ref, w_ref, o_ref, *, nq, nqk):
    j = pl.program_id(0)
    i = pl.program_id(1)
    x = x_ref[...].astype(F32)
    tm, tc = x.shape
    prev = jnp.where(i > 0, p_ref[...].astype(F32), 0.0)
    w = w_ref[...]
    row8 = lax.broadcasted_iota(jnp.int32, (8, tc), 0)
    acc = x * w[DN_CONV - 1:DN_CONV, :]
    for s in range(1, DN_CONV):
        r = pltpu.roll(x, s, axis=0)
        pr = pltpu.roll(prev, s, axis=0)
        top = jnp.where(row8 < s, pr[:8], r[:8])
        xs = jnp.concatenate([top, r[8:]], axis=0)
        acc = acc + xs * w[DN_CONV - 1 - s:DN_CONV - s, :]
    y = acc * _sigmoid(acc)
    qscale = jnp.where(j < nq, DN_HD ** -0.5, 1.0)
    for hh in range(tc // DN_HD):
        blk = y[:, hh * DN_HD:(hh + 1) * DN_HD]
        ss = jnp.sum(blk * blk, axis=-1, keepdims=True)
        fac = jnp.where(j < nqk, lax.rsqrt(ss + L2_EPS) * qscale, 1.0)
        o_ref[:, hh * DN_HD:(hh + 1) * DN_HD] = (blk * fac).astype(o_ref.dtype)


def _dn_conv(proj, conv_wt, tm=512, tc=512):
    t = proj.shape[0]
    tm = min(tm, t)
    pb = tm // 16
    return pl.pallas_call(
        functools.partial(_conv_kernel, nq=DN_KEY_W // tc, nqk=2 * DN_KEY_W // tc),
        grid=(DN_CONV_CH // tc, t // tm),
        in_specs=[pl.BlockSpec((tm, tc), lambda j, i: (i, j)),
                  pl.BlockSpec((16, tc), lambda j, i: (jnp.maximum(i * pb - 1, 0), j)),
                  pl.BlockSpec((DN_CONV, tc), lambda j, i: (0, j))],
        out_specs=pl.BlockSpec((tm, tc), lambda j, i: (i, j)),
        out_shape=jax.ShapeDtypeStruct((t, DN_CONV_CH), BF16),
        compiler_params=_cparams(("arbitrary", "arbitrary")),
    )(proj, proj, conv_wt)


def _dn_gate_kernel(x_ref, a_ref, dtb_ref, o_ref):
    x = x_ref[...]
    tm, w = x.shape
    lane = lax.broadcasted_iota(jnp.int32, (tm, w), 1)
    row = lax.broadcasted_iota(jnp.int32, (tm, w), 0)
    xs = x + dtb_ref[...]
    softplus = jnp.maximum(xs, 0.0) + jnp.log(1.0 + jnp.exp(-jnp.abs(xs)))
    g = -jnp.exp(a_ref[...]) * softplus
    pos = row & (DN_C - 1)
    d = 1
    while d < DN_C:
        g = g + jnp.where(pos >= d, pltpu.roll(g, d, axis=0), 0.0)
        d *= 2
    o_ref[...] = jnp.where(lane < DN_V_HEADS, _sigmoid(x), g)


def _dn_gates(bd, a_log, dt_bias, tm=512):
    t = bd.shape[0]
    tm = min(tm, t)
    zeros = jnp.zeros((DN_V_HEADS,), F32)
    a2 = jnp.concatenate([zeros, a_log]).reshape(1, -1)
    b2 = jnp.concatenate([zeros, dt_bias]).reshape(1, -1)
    w = 2 * DN_V_HEADS
    return pl.pallas_call(
        _dn_gate_kernel,
        grid=(t // tm,),
        in_specs=[pl.BlockSpec((tm, w), lambda i: (i, 0)),
                  pl.BlockSpec((1, w), lambda i: (0, 0)),
                  pl.BlockSpec((1, w), lambda i: (0, 0))],
        out_specs=pl.BlockSpec((tm, w), lambda i: (i, 0)),
        out_shape=jax.ShapeDtypeStruct((t, w), F32),
        compiler_params=_cparams(("arbitrary",)),
    )(bd, a2, b2)


def _bdot(a, b):
    return jnp.einsum('bij,bjk->bik', a.astype(BF16), b.astype(BF16), preferred_element_type=F32)


def _unit_lower_inverse(a, ii, jj):
    c = a.shape[-1]
    eye = (ii == jj).astype(F32)
    p = jnp.where((ii >> 3) == (jj >> 3), -a, 0.0)
    p2 = _bdot(p, p)
    t = eye + p
    t = t + _bdot(t, p2)
    p4 = _bdot(p2, p2)
    t = t + _bdot(t, p4)
    s = 8
    sh = 3
    while s < c:
        bi = ii >> sh
        bj = jj >> sh
        off = jnp.where(((bi & 1) == 1) & (bj == bi - 1), a, 0.0)
        t = t - _bdot(_bdot(t, off), t)
        s *= 2
        sh += 1
    return t


def _delta_kernel(q_ref, k_ref, v_ref, z_ref, bg_ref, gr_ref, nw_ref, o_ref, s_ref, *, nchunk, nh):
    pp = pl.program_id(0)
    i = pl.program_id(1)

    @pl.when(i == 0)
    def _():
        s_ref[...] = jnp.zeros_like(s_ref)

    c = DN_C
    hd = DN_HD
    bg = bg_ref[...]
    lane = lax.broadcasted_iota(jnp.int32, bg.shape, 1)
    ii = lax.broadcasted_iota(jnp.int32, (c, c), 0)
    jj = lax.broadcasted_iota(jnp.int32, (c, c), 1)
    nw = nw_ref[...]

    q3 = q_ref[...].reshape(nchunk, c, hd)
    k3 = k_ref[...].reshape(nchunk, c, hd)
    kk = jnp.einsum('cid,cjd->cij', k3, k3, preferred_element_type=F32)
    qk = jnp.einsum('cid,cjd->cij', q3, k3, preferred_element_type=F32)
    kf = k3.astype(F32)
    qf = q3.astype(F32)
    a_l, attn_l, rhs_l, qd_l, kd_l, gl_l = [], [], [], [], [], []
    for j in range(nh):
        hh = nh * pp + j
        beta = jnp.sum(jnp.where(lane == hh, bg, 0.0), axis=1, keepdims=True).reshape(nchunk, c, 1)
        gc = jnp.sum(jnp.where(lane == hh + DN_V_HEADS, bg, 0.0), axis=1,
                     keepdims=True).reshape(nchunk, c, 1)
        gr = gr_ref[j]
        decay = jnp.exp(jnp.where(ii >= jj, gc - gr, -jnp.inf))
        a_l.append(jnp.where(ii > jj, kk * decay, 0.0) * beta)
        attn_l.append((qk * decay).astype(BF16))
        eg = jnp.exp(gc)
        v = v_ref[:, j * hd:(j + 1) * hd].astype(F32).reshape(nchunk, c, hd)
        rhs_l.append(jnp.concatenate([v * beta, kf * (beta * eg)], axis=-1))
        qd_l.append((qf * eg).astype(BF16))
        g_last = gr[:, :, c - 1:c]
        kd_l.append(kf * jnp.exp(g_last - gc))
        gl_l.append(jnp.exp(g_last))
    tinv = _unit_lower_inverse(jnp.concatenate(a_l, axis=0), ii, jj)
    sol = _bdot(tinv, jnp.concatenate(rhs_l, axis=0))

    s = [s_ref[j] for j in range(nh)]
    for ci in range(nchunk):
        r0 = ci * c
        for j in range(nh):
            b = j * nchunk + ci
            u_c = sol[b, :, :hd]
            w_c = sol[b, :, hd:]
            sb = s[j].astype(BF16)
            v_new = u_c - jnp.dot(w_c.astype(BF16), sb, preferred_element_type=F32)
            vb = v_new.astype(BF16)
            o = (jnp.dot(qd_l[j][ci], sb, preferred_element_type=F32)
                 + jnp.dot(attn_l[j][ci], vb, preferred_element_type=F32))
            s[j] = s[j] * gl_l[j][ci] + jnp.dot(kd_l[j][ci].T.astype(BF16), vb,
                                               preferred_element_type=F32)
            ms = jnp.mean(o * o, axis=-1, keepdims=True)
            z = z_ref[r0:r0 + c, j * hd:(j + 1) * hd].astype(F32)
            o = o * lax.rsqrt(ms + RMS_EPS) * nw * (z * _sigmoid(z))
            o_ref[r0:r0 + c, j * hd:(j + 1) * hd] = o.astype(o_ref.dtype)
    for j in range(nh):
        s_ref[j] = s[j]


def _delta_rule(qkv, proj, bg, gc_rows, norm_w, rb=1024):
    t = qkv.shape[0]
    rb = min(rb, t)
    nchunk = rb // DN_C
    nh = DN_V_HEADS // DN_QK_HEADS
    kq = DN_KEY_W // DN_HD
    voff = 2 * DN_KEY_W // (nh * DN_HD)
    zoff = DN_CONV_CH // (nh * DN_HD)
    return pl.pallas_call(
        functools.partial(_delta_kernel, nchunk=nchunk, nh=nh),
        grid=(DN_QK_HEADS, t // rb),
        in_specs=[pl.BlockSpec((rb, DN_HD), lambda p, i: (i, p)),
                  pl.BlockSpec((rb, DN_HD), lambda p, i: (i, kq + p)),
                  pl.BlockSpec((rb, nh * DN_HD), lambda p, i: (i, voff + p)),
                  pl.BlockSpec((rb, nh * DN_HD), lambda p, i: (i, zoff + p)),
                  pl.BlockSpec((rb, 2 * DN_V_HEADS), lambda p, i: (i, 0)),
                  pl.BlockSpec((nh, nchunk, 1, DN_C), lambda p, i: (p, i, 0, 0)),
                  pl.BlockSpec((1, DN_HD), lambda p, i: (0, 0))],
        out_specs=pl.BlockSpec((rb, nh * DN_HD), lambda p, i: (i, p)),
        out_shape=jax.ShapeDtypeStruct((t, DN_VAL_W), BF16),
        scratch_shapes=[pltpu.VMEM((nh, DN_HD, DN_HD), F32)],
        compiler_params=_cparams(("arbitrary", "arbitrary")),
    )(qkv, qkv, qkv, proj, bg, gc_rows, norm_w.reshape(1, DN_HD))


def _router_kernel(h_ref, w_ref, b_ref, idx_ref, wt_ref):
    logits = _dot3(h_ref[...], w_ref[...]) + b_ref[...]
    tm, e = logits.shape
    lane = lax.broadcasted_iota(jnp.int32, (tm, e), 1).astype(F32)
    lane_o = lax.broadcasted_iota(jnp.int32, (tm, LANE), 1)
    idx_out = jnp.zeros((tm, LANE), jnp.int32)
    val_out = jnp.zeros((tm, LANE), F32)
    cur = logits
    vals = []
    for kth in range(TOP_K):
        m = jnp.max(cur, axis=-1, keepdims=True)
        sel = jnp.min(jnp.where(cur == m, lane, float(e)), axis=-1, keepdims=True)
        cur = jnp.where(lane == sel, -jnp.inf, cur)
        idx_out = jnp.where(lane_o == kth, sel.astype(jnp.int32), idx_out)
        vals.append(m)
    es = [jnp.exp(v - vals[0]) for v in vals]
    tot = es[0]
    for x in es[1:]:
        tot = tot + x
    for kth in range(TOP_K):
        val_out = jnp.where(lane_o == kth, es[kth] / tot, val_out)
    idx_ref[...] = idx_out
    wt_ref[...] = val_out


def _router(h, w_router, b_router, tm=512):
    t, d = h.shape
    tm = min(tm, t)
    e = w_router.shape[1]
    return pl.pallas_call(
        _router_kernel,
        grid=(t // tm,),
        in_specs=[pl.BlockSpec((tm, d), lambda i: (i, 0)),
                  pl.BlockSpec((d, e), lambda i: (0, 0)),
                  pl.BlockSpec((1, e), lambda i: (0, 0))],
        out_specs=[pl.BlockSpec((tm, LANE), lambda i: (i, 0)),
                   pl.BlockSpec((tm, LANE), lambda i: (i, 0))],
        out_shape=[jax.ShapeDtypeStruct((t, LANE), jnp.int32),
                   jax.ShapeDtypeStruct((t, LANE), F32)],
        compiler_params=_cparams(("arbitrary",)),
    )(h, w_router, b_router.reshape(1, e))


def _expert_kernel(be_ref, na_ref, x_ref, wg_ref, wu_ref, wd_ref, bg_ref, bu_ref, bd_ref,
                   o_ref, wgb_ref, wub_ref, wdb_ref):
    i = pl.program_id(0)
    prev = be_ref[jnp.maximum(i - 1, 0)]
    changed = jnp.logical_or(i == 0, be_ref[i] != prev)

    @pl.when(changed)
    def _():
        wgb_ref[...] = wg_ref[...].astype(BF16)
        wub_ref[...] = wu_ref[...].astype(BF16)
        wdb_ref[...] = wd_ref[...].astype(BF16)

    @pl.when(i < na_ref[0])
    def _():
        x = x_ref[...]
        gate = jnp.minimum(jnp.dot(x, wgb_ref[...], preferred_element_type=F32) + bg_ref[...],
                           SWIGLU_LIMIT)
        up = jnp.clip(jnp.dot(x, wub_ref[...], preferred_element_type=F32) + bu_ref[...],
                      -SWIGLU_LIMIT, SWIGLU_LIMIT)
        hid = gate * _sigmoid(SWIGLU_ALPHA * gate) * (up + 1.0)
        y = jnp.dot(hid.astype(BF16), wdb_ref[...], preferred_element_type=F32) + bd_ref[...]
        o_ref[...] = y.astype(o_ref.dtype)

    @pl.when(i >= na_ref[0])
    def _():
        o_ref[...] = jnp.zeros_like(o_ref)


def _experts(xs, block_e, n_active, layer, w_gate, b_gate, w_up, b_up, w_down, b_down):
    n_slots, d = xs.shape
    nb = n_slots // MOE_BM
    f = w_gate.shape[-1]
    l = layer

    def wspec(shape):
        return pl.BlockSpec((None, None) + shape, lambda i, be, na: (l, be[i], 0, 0))

    grid_spec = pltpu.PrefetchScalarGridSpec(
        num_scalar_prefetch=2,
        grid=(nb,),
        in_specs=[pl.BlockSpec((MOE_BM, d), lambda i, be, na: (i, 0)),
                  wspec((d, f)), wspec((d, f)), wspec((f, d)),
                  wspec((1, f)), wspec((1, f)), wspec((1, d))],
        out_specs=pl.BlockSpec((MOE_BM, d), lambda i, be, na: (i, 0)),
        scratch_shapes=[pltpu.VMEM((d, f), BF16), pltpu.VMEM((d, f), BF16), pltpu.VMEM((f, d), BF16)],
    )
    nl, ne = b_gate.shape[:2]
    return pl.pallas_call(
        _expert_kernel,
        grid_spec=grid_spec,
        out_shape=jax.ShapeDtypeStruct((n_slots, d), F32),
        compiler_params=_cparams(("arbitrary",)),
    )(block_e, n_active, xs, w_gate, w_up, w_down,
      b_gate.reshape(nl, ne, 1, f), b_up.reshape(nl, ne, 1, f), b_down.reshape(nl, ne, 1, d))


def _routing(top_idx):
    t = top_idx.shape[0]
    sel = jnp.sum(jax.nn.one_hot(top_idx, N_EXPERTS, dtype=jnp.int32), axis=1)
    counts = jnp.sum(sel, axis=0)
    before = jnp.cumsum(sel, axis=0) - sel
    padded = (counts + MOE_BM - 1) // MOE_BM * MOE_BM
    pend = jnp.cumsum(padded)
    pstart = pend - padded
    dest = pstart[top_idx] + jnp.take_along_axis(before, top_idx, axis=1)
    n_blocks = t * TOP_K // MOE_BM + N_EXPERTS
    n_active = (pend[-1] // MOE_BM).astype(jnp.int32)
    blk = jnp.arange(n_blocks, dtype=jnp.int32)
    blk = jnp.minimum(blk, jnp.maximum(n_active - 1, 0))
    block_e = jnp.sum((blk[:, None] * MOE_BM >= pend[None, :]).astype(jnp.int32), axis=1)
    block_e = jnp.minimum(block_e, N_EXPERTS - 1).astype(jnp.int32)
    tok = jnp.broadcast_to(jnp.arange(t, dtype=jnp.int32)[:, None], dest.shape)
    slot_tok = jnp.zeros((n_blocks * MOE_BM,), jnp.int32).at[dest.reshape(-1)].set(tok.reshape(-1))
    return dest.astype(jnp.int32), slot_tok, block_e, n_active.reshape(1)


def _combine_kernel(dest_ref, h_ref, w_ref, g_ref, b_ref, ys_ref, o_ref, obf_ref, buf, sem, *, tm):
    i = pl.program_id(0)
    n = pl.num_programs(0)

    def row_copy(blk, slot, r, k):
        d = dest_ref[(blk * tm + r) * TOP_K + k]
        return pltpu.make_async_copy(ys_ref.at[pl.ds(d, 1)], buf.at[slot, k, pl.ds(r, 1)],
                                     sem.at[slot])

    def start_block(blk, slot):
        def body(r, carry):
            for k in range(TOP_K):
                row_copy(blk, slot, r, k).start()
            return carry
        lax.fori_loop(0, tm, body, 0)

    def wait_block(blk, slot):
        def body(r, carry):
            for k in range(TOP_K):
                row_copy(blk, slot, r, k).wait()
            return carry
        lax.fori_loop(0, tm, body, 0)

    @pl.when(i == 0)
    def _():
        start_block(0, 0)

    @pl.when(i + 1 < n)
    def _():
        start_block(i + 1, (i + 1) % 2)

    slot = i % 2
    wait_block(i, slot)
    w = w_ref[...]
    ffn = w[:, 0:1] * buf[slot, 0]
    for k in range(1, TOP_K):
        ffn = ffn + w[:, k:k + 1] * buf[slot, k]
    y = _layer_norm_rows(ALPHA * h_ref[...] + ffn, g_ref[...], b_ref[...])
    o_ref[...] = y
    obf_ref[...] = y.astype(BF16)


def _combine_norm(dest, h, top_w, ys, g, b, tm=128):
    t, d = h.shape
    tm = min(tm, t)
    row = pl.BlockSpec((tm, d), lambda i, dref: (i, 0))
    vec = pl.BlockSpec((1, d), lambda i, dref: (0, 0))
    grid_spec = pltpu.PrefetchScalarGridSpec(
        num_scalar_prefetch=1,
        grid=(t // tm,),
        in_specs=[row, pl.BlockSpec((tm, LANE), lambda i, dref: (i, 0)), vec, vec,
                  pl.BlockSpec(memory_space=pl.ANY)],
        out_specs=[row, row],
        scratch_shapes=[pltpu.VMEM((2, TOP_K, tm, d), F32), pltpu.SemaphoreType.DMA((2,))],
    )
    return pl.pallas_call(
        functools.partial(_combine_kernel, tm=tm),
        grid_spec=grid_spec,
        out_shape=[jax.ShapeDtypeStruct((t, d), F32), jax.ShapeDtypeStruct((t, d), BF16)],
        compiler_params=_cparams(("arbitrary",)),
    )(dest.reshape(-1), h, top_w, g.reshape(1, d), b.reshape(1, d), ys)


def _layer(l, h, hb, p):
    t = h.shape[0]
    w_in = p['w_in']
    u = _matmul(hb, [(w_in, (l,), 0)], D_MODEL, F32, lambda acc: acc)
    proj = _matmul(hb, [(w_in, (l,), OFF_QKV // 512)], OFF_BETA - OFF_QKV, BF16, lambda acc: acc)
    gates = _matmul(hb, [(w_in[l, :, OFF_GATE_S5:], (), 0)], 2 * D_MODEL, BF16,
                    lambda acc: _sigmoid(acc))
    bd = _matmul(hb, [(w_in[l, :, OFF_BETA:OFF_GATE_S5], (), 0)], 2 * DN_V_HEADS, F32,
                 lambda acc: acc, tn=2 * DN_V_HEADS)

    tables = _s5_tables(p['s5_lam_re'][l], p['s5_lam_im'][l], p['s5_log_dt'][l], p['s5_b_re'][l],
                        p['s5_b_im'][l], p['s5_c_re'][l], p['s5_c_im'][l], p['s5_d'][l], t // S5_L)
    y = _s5_apply(u, tables)
    part = _matmul(y, [(p['w_glu_a'], (l,), 0), (p['w_glu_b'], (l,), 0)], D_MODEL, BF16,
                   lambda a, b, g: a * _sigmoid(b) * g.astype(F32), extras=[(gates, 0)])

    qkv = _dn_conv(proj, p['dn_conv_w'][l].T)
    bg = _dn_gates(bd, p['dn_a_log'][l], p['dn_dt_bias'][l])
    gc_rows = bg[:, DN_V_HEADS:].T.reshape(DN_V_HEADS, t // DN_C, 1, DN_C)
    o = _delta_rule(qkv, proj, bg, gc_rows, p['dn_norm_w'][l])
    merged = _matmul(o, [(p['w_dn_out'], (l,), 0)], D_MODEL, BF16,
                     lambda acc, g, s: acc * g.astype(F32) + s.astype(F32),
                     extras=[(gates, D_MODEL // 512), (part, 0)])
    mix = _matmul(merged, [(p['w_mix_out'], (l,), 0)], D_MODEL, F32, lambda acc: acc)
    h, hb = _deepnorm(h, mix, p['ln1_g'][l], p['ln1_b'][l])

    top_idx, top_w = _router(h, p['w_router'][l], p['b_router'][l])
    dest, slot_tok, block_e, n_active = _routing(top_idx[:, :TOP_K])
    xs = jnp.take(hb, slot_tok, axis=0)
    ys = _experts(xs, block_e, n_active, l, p['w_gate'], p['b_gate'], p['w_up'], p['b_up'],
                  p['w_down'], p['b_down'])
    return _combine_norm(dest, h, top_w, ys, p['ln2_g'][l], p['ln2_b'][l])


def kernel(x, w_in, dn_conv_w, dn_a_log, dn_dt_bias, dn_norm_w, w_dn_out, s5_lam_re, s5_lam_im, s5_log_dt, s5_b_re, s5_b_im, s5_c_re, s5_c_im, s5_d, w_glu_a, w_glu_b, w_mix_out, ln1_g, ln1_b, w_router, b_router, w_gate, b_gate, w_up, b_up, w_down, b_down, ln2_g, ln2_b):
    p = dict(w_in=w_in, dn_conv_w=dn_conv_w, dn_a_log=dn_a_log, dn_dt_bias=dn_dt_bias,
             dn_norm_w=dn_norm_w, w_dn_out=w_dn_out, s5_lam_re=s5_lam_re, s5_lam_im=s5_lam_im,
             s5_log_dt=s5_log_dt, s5_b_re=s5_b_re, s5_b_im=s5_b_im, s5_c_re=s5_c_re,
             s5_c_im=s5_c_im, s5_d=s5_d, w_glu_a=w_glu_a, w_glu_b=w_glu_b, w_mix_out=w_mix_out,
             ln1_g=ln1_g, ln1_b=ln1_b, w_router=w_router, b_router=b_router, w_gate=w_gate,
             b_gate=b_gate, w_up=w_up, b_up=b_up, w_down=w_down, b_down=b_down,
             ln2_g=ln2_g, ln2_b=ln2_b)
    bsz, t, d = x.shape
    h = x.reshape(bsz * t, d)
    hb = h.astype(BF16)
    for l in range(w_in.shape[0]):
        h, hb = _layer(l, h, hb, p)
    return h.reshape(bsz, t, d)
```

```python
import functools
import math

import jax
import jax.numpy as jnp
from jax import lax
from jax.experimental import pallas as pl
from jax.experimental.pallas import tpu as pltpu

F32 = jnp.float32
BF16 = jnp.bfloat16

D_MODEL = 2048
DEPTH = 4
S5_P = 16
S5_N = 64
S5_G = D_MODEL // S5_P
S5_L = 16
DN_QK_HEADS = 16
DN_V_HEADS = 32
DN_HD = 128
DN_KEY_W = DN_QK_HEADS * DN_HD
DN_VAL_W = DN_V_HEADS * DN_HD
DN_CONV_CH = 2 * DN_KEY_W + DN_VAL_W
DN_CONV = 4
DN_C = 128
N_EXPERTS = 32
TOP_K = 4
D_EXPERT = 512
SWIGLU_LIMIT = 7.0
SWIGLU_ALPHA = 1.702
MOE_BM = 256
ALPHA = (2 * DEPTH) ** 0.25
LN_EPS = 1e-5
RMS_EPS = 1e-6
L2_EPS = 1e-6
S5_MAX_REAL = -1e-4
OFF_QKV = D_MODEL
OFF_Z = OFF_QKV + DN_CONV_CH
OFF_BETA = OFF_Z + DN_VAL_W
OFF_GATE_S5 = OFF_BETA + 2 * DN_V_HEADS
IN_COLS = OFF_GATE_S5 + 2 * D_MODEL
LANE = 128
VMEM_LIMIT = 56 * 1024 * 1024


def _cparams(sem):
    return pltpu.CompilerParams(dimension_semantics=sem, vmem_limit_bytes=VMEM_LIMIT)


def _mm_kernel(*refs, nw, nx, epilogue):
    a_ref = refs[0]
    w_refs = refs[1:1 + nw]
    x_refs = refs[1 + nw:1 + nw + nx]
    o_ref = refs[1 + nw + nx]
    wbf_refs = refs[2 + nw + nx:]

    @pl.when(pl.program_id(1) == 0)
    def _():
        for w_ref, wbf_ref in zip(w_refs, wbf_refs):
            wbf_ref[...] = w_ref[...].astype(BF16)

    a = a_ref[...].astype(BF16)
    accs = [jnp.dot(a, wbf[...], preferred_element_type=F32) for wbf in wbf_refs]
    o_ref[...] = epilogue(*accs, *[x[...] for x in x_refs]).astype(o_ref.dtype)


def _matmul(a, ws, n_out, out_dtype, epilogue, extras=(), tm=512, tn=512):
    m, k = a.shape
    tm = min(tm, m)
    grid = (n_out // tn, m // tm)
    in_specs = [pl.BlockSpec((tm, k), lambda j, i: (i, 0))]
    args = [a]
    for w, lead, off in ws:
        nlead = len(lead)
        in_specs.append(pl.BlockSpec((None,) * nlead + (k, tn),
                                     lambda j, i, lead=lead, off=off: tuple(lead) + (0, j + off)))
        args.append(w)
    for x, off in extras:
        in_specs.append(pl.BlockSpec((tm, tn), lambda j, i, off=off: (i, j + off)))
        args.append(x)
    return pl.pallas_call(
        functools.partial(_mm_kernel, nw=len(ws), nx=len(extras), epilogue=epilogue),
        grid=grid,
        in_specs=in_specs,
        out_specs=pl.BlockSpec((tm, tn), lambda j, i: (i, j)),
        out_shape=jax.ShapeDtypeStruct((m, n_out), out_dtype),
        scratch_shapes=[pltpu.VMEM((k, tn), BF16) for _ in ws],
        compiler_params=_cparams(("arbitrary", "arbitrary")),
    )(*args)


def _sigmoid(x):
    return 1.0 / (1.0 + jnp.exp(-x))


def _split_bf16(x):
    hi = x.astype(BF16)
    lo = (x - hi.astype(F32)).astype(BF16)
    return hi, lo


def _dot3(a, b, dims=(((1,), (0,)), ((), ()))):
    ah, al = _split_bf16(a)
    bh, bl = _split_bf16(b)
    return (lax.dot_general(ah, bh, dims, preferred_element_type=F32)
            + lax.dot_general(ah, bl, dims, preferred_element_type=F32)
            + lax.dot_general(al, bh, dims, preferred_element_type=F32))


_NT = (((1,), (1,)), ((), ()))


def _layer_norm_rows(x, g, b):
    mu = jnp.mean(x, axis=-1, keepdims=True)
    xc = x - mu
    var = jnp.mean(xc * xc, axis=-1, keepdims=True)
    return xc * lax.rsqrt(var + LN_EPS) * g + b


def _ln_kernel(h_ref, r_ref, g_ref, b_ref, o_ref, obf_ref):
    y = _layer_norm_rows(ALPHA * h_ref[...] + r_ref[...], g_ref[...], b_ref[...])
    o_ref[...] = y
    obf_ref[...] = y.astype(BF16)


def _deepnorm(h, r, g, b, tm=256):
    t, d = h.shape
    tm = min(tm, t)
    row = pl.BlockSpec((tm, d), lambda i: (i, 0))
    vec = pl.BlockSpec((1, d), lambda i: (0, 0))
    return pl.pallas_call(
        _ln_kernel,
        grid=(t // tm,),
        in_specs=[row, row, vec, vec],
        out_specs=[row, row],
        out_shape=[jax.ShapeDtypeStruct((t, d), F32), jax.ShapeDtypeStruct((t, d), BF16)],
        compiler_params=_cparams(("arbitrary",)),
    )(h, r, g.reshape(1, d), b.reshape(1, d))


def _s5_tables(lam_re, lam_im, log_dt, b_re, b_im, c_re, c_im, d_skip, n_chunks):
    L = S5_L
    g = lam_re.shape[0]
    lre = jnp.minimum(lam_re, S5_MAX_REAL)
    lim = lam_im
    dt = jnp.exp(log_dt)[:, None]
    ks = jnp.arange(L + 1, dtype=F32)[:, None, None]
    mag = jnp.exp(lre * dt * ks)
    pr = mag * jnp.cos(lim * dt * ks)
    pi = mag * jnp.sin(lim * dt * ks)
    a_re, a_im = pr[1], pi[1]
    den = lre * lre + lim * lim
    f_re = ((a_re - 1.0) * lre + a_im * lim) / den
    f_im = (a_im * lre - (a_re - 1.0) * lim) / den
    bb_re = f_re[..., None] * b_re - f_im[..., None] * b_im
    bb_im = f_re[..., None] * b_im + f_im[..., None] * b_re
    bb = jnp.concatenate([jnp.transpose(bb_re, (0, 2, 1)), jnp.transpose(bb_im, (0, 2, 1))], axis=-1)
    cc = jnp.concatenate([c_re, c_im], axis=-1)
    qr = jnp.transpose(pr[L - 1::-1], (1, 0, 2))
    qi = jnp.transpose(pi[L - 1::-1], (1, 0, 2))
    p1 = jnp.concatenate([qr, qr], axis=-1)
    p2 = jnp.concatenate([-qi, qi], axis=-1)
    ur = jnp.transpose(pr[1:], (1, 0, 2))
    ui = jnp.transpose(pi[1:], (1, 0, 2))
    q1 = jnp.concatenate([ur, -ur], axis=-1)
    q2 = jnp.concatenate([-ui, -ui], axis=-1)
    nlev = max(1, int(math.log2(n_chunks)))
    mr, mi = pr[L], pi[L]
    m1, m2 = [], []
    for _ in range(nlev):
        m1.append(jnp.concatenate([mr, mr], axis=-1))
        m2.append(jnp.concatenate([-mi, mi], axis=-1))
        mr, mi = mr * mr - mi * mi, 2.0 * mr * mi
    pad = [jnp.zeros_like(m1[0])] * (16 - nlev)
    pw1 = jnp.stack(m1 + pad, axis=1)
    pw2 = jnp.stack(m2 + pad, axis=1)
    dsk = jnp.tile(d_skip, (1, L)).reshape(g, 1, L * S5_P)
    return p1, p2, q1, q2, bb, cc, pw1, pw2, dsk


def _gelu_tanh(y):
    return 0.5 * y * (1.0 + jnp.tanh(0.7978845608028654 * (y + 0.044715 * y * y * y)))


def _rep_rows(x, n):
    r, w = x.shape
    return jnp.broadcast_to(x[:, None, :], (r, n, w)).reshape(r * n, w)


def _tile_rows(x, n):
    r, w = x.shape
    return jnp.broadcast_to(x[None, :, :], (n, r, w)).reshape(n * r, w)


def _s5_kernel(u_ref, p1_ref, p2_ref, q1_ref, q2_ref, bb_ref, cc_ref, pw1_ref, pw2_ref, d_ref,
               o_ref, perm_ref, *, nlev, gb):
    L, P, N = S5_L, S5_P, S5_N
    lp = L * P
    half = (L // 2) * LANE
    nc = u_ref.shape[0] // L

    @pl.when(pl.program_id(0) == 0)
    def _():
        r = lax.broadcasted_iota(jnp.int32, (half, half), 0)
        c = lax.broadcasted_iota(jnp.int32, (half, half), 1)
        dst = ((r >> 4) & 7) * LANE + (r >> 7) * P + (r & 15)
        perm_ref[...] = jnp.where(c == dst, 1.0, 0.0).astype(BF16)

    perm = perm_ref[...]
    v = []
    for th in range(2):
        xs = [u_ref[pl.ds(th * 8 + tl, nc, stride=L), :].astype(BF16) for tl in range(8)]
        v.append(jnp.dot(jnp.concatenate(xs, axis=1), perm, preferred_element_type=F32).astype(BF16))

    row = lax.broadcasted_iota(jnp.int32, (nc, 2 * N), 0)
    lane_n = lax.broadcasted_iota(jnp.int32, (1, 2 * N), 1)
    sign = jnp.where(lane_n < N, 1.0, -1.0)
    rblk = lax.broadcasted_iota(jnp.int32, (lp, lp), 0) >> 4
    cblk = lax.broadcasted_iota(jnp.int32, (lp, lp), 1) >> 4
    z = [[], []]
    for gi in range(gb):
        ug = jnp.concatenate([v[0][:, gi * LANE:(gi + 1) * LANE],
                              v[1][:, gi * LANE:(gi + 1) * LANE]], axis=1)
        bbg = bb_ref[gi]
        ccg = cc_ref[gi]
        bmat = (_rep_rows(p1_ref[gi], P) * _tile_rows(bbg, L)
                + _rep_rows(p2_ref[gi], P) * _tile_rows(pltpu.roll(bbg, N, axis=1), L))
        cmt = (_rep_rows(q1_ref[gi], P) * _tile_rows(ccg, L)
               + _rep_rows(q2_ref[gi], P) * _tile_rows(pltpu.roll(ccg, N, axis=1), L))
        w = _dot3(bmat, _tile_rows(ccg * sign, L), _NT)
        tmat = jnp.zeros((lp, lp), F32)
        for t in range(L):
            shift = (lp - (L - 1 - t) * P) % lp
            tmat = jnp.where((cblk == t) & (rblk <= t), pltpu.roll(w, shift, axis=0), tmat)
        x = jnp.dot(ug, bmat.astype(BF16), preferred_element_type=F32)
        for lev in range(nlev):
            d = 1 << lev
            m1 = pw1_ref[gi, lev:lev + 1, :]
            m2 = pw2_ref[gi, lev:lev + 1, :]
            sh = jnp.where(row >= d, pltpu.roll(x, d, axis=0), 0.0)
            x = x + m1 * sh + m2 * pltpu.roll(sh, N, axis=1)
        sprev = jnp.where(row >= 1, pltpu.roll(x, 1, axis=0), 0.0)
        y = (jnp.dot(ug, tmat.astype(BF16), preferred_element_type=F32)
             + lax.dot_general(sprev.astype(BF16), cmt.astype(BF16), _NT, preferred_element_type=F32)
             + d_ref[gi] * ug.astype(F32))
        yg = _gelu_tanh(y).astype(BF16)
        z[0].append(yg[:, :LANE])
        z[1].append(yg[:, LANE:])
    for th in range(2):
        yp = lax.dot_general(jnp.concatenate(z[th], axis=1), perm, _NT, preferred_element_type=F32)
        for tl in range(8):
            o_ref[pl.ds(th * 8 + tl, nc, stride=L), :] = yp[:, tl * LANE:(tl + 1) * LANE]


def _s5_apply(u, tables):
    t, d = u.shape
    gb = LANE // S5_P
    nc = t // S5_L
    nlev = int(math.log2(nc))
    assert (1 << nlev) == nc and nlev <= 16
    tab = pl.BlockSpec((gb, 16, 2 * S5_N), lambda i: (i, 0, 0))
    return pl.pallas_call(
        functools.partial(_s5_kernel, nlev=nlev, gb=gb),
        grid=(d // LANE,),
        in_specs=[pl.BlockSpec((t, LANE), lambda i: (0, i))] + [tab] * 8
                 + [pl.BlockSpec((gb, 1, S5_L * S5_P), lambda i: (i, 0, 0))],
        out_specs=pl.BlockSpec((t, LANE), lambda i: (0, i)),
        out_shape=jax.ShapeDtypeStruct((t, d), F32),
        scratch_shapes=[pltpu.VMEM((8 * LANE, 8 * LANE), BF16)],
        compiler_params=_cparams(("arbitrary",)),
    )(u, *tables)


def _conv_kernel(x_ref, p_ref, w_ref, o_ref, xs_ref, *, nq, nqk):
    j = pl.program_id(0)
    i = pl.program_id(1)
    tm, tc = x_ref.shape
    prev = p_ref[...].astype(F32)
    xs_ref[0:8, :] = jnp.where(i > 0, prev[8:16], 0.0)
    xs_ref[8:8 + tm, :] = x_ref[...].astype(F32)
    w = w_ref[...]
    acc = xs_ref[8:8 + tm, :] * w[DN_CONV - 1:DN_CONV, :]
    for s in range(1, DN_CONV):
        acc = acc + xs_ref[8 - s:8 - s + tm, :] * w[DN_CONV - 1 - s:DN_CONV - s, :]
    y = acc * _sigmoid(acc)
    qscale = jnp.where(j < nq, DN_HD ** -0.5, 1.0)
    for hh in range(tc // DN_HD):
        blk = y[:, hh * DN_HD:(hh + 1) * DN_HD]
        ss = jnp.sum(blk * blk, axis=-1, keepdims=True)
        fac = jnp.where(j < nqk, lax.rsqrt(ss + L2_EPS) * qscale, 1.0)
        o_ref[:, hh * DN_HD:(hh + 1) * DN_HD] = (blk * fac).astype(o_ref.dtype)


def _dn_conv(proj, conv_wt, tm=512, tc=512):
    t = proj.shape[0]
    tm = min(tm, t)
    pb = tm // 16
    return pl.pallas_call(
        functools.partial(_conv_kernel, nq=DN_KEY_W // tc, nqk=2 * DN_KEY_W // tc),
        grid=(DN_CONV_CH // tc, t // tm),
        in_specs=[pl.BlockSpec((tm, tc), lambda j, i: (i, j)),
                  pl.BlockSpec((16, tc), lambda j, i: (jnp.maximum(i * pb - 1, 0), j)),
                  pl.BlockSpec((DN_CONV, tc), lambda j, i: (0, j))],
        out_specs=pl.BlockSpec((tm, tc), lambda j, i: (i, j)),
        out_shape=jax.ShapeDtypeStruct((t, DN_CONV_CH), BF16),
        scratch_shapes=[pltpu.VMEM((tm + 8, tc), F32)],
        compiler_params=_cparams(("arbitrary", "arbitrary")),
    )(proj, proj, conv_wt)


def _dn_gate_kernel(x_ref, a_ref, dtb_ref, o_ref):
    x = x_ref[...]
    tm, w = x.shape
    lane = lax.broadcasted_iota(jnp.int32, (tm, w), 1)
    row = lax.broadcasted_iota(jnp.int32, (tm, w), 0)
    xs = x + dtb_ref[...]
    softplus = jnp.maximum(xs, 0.0) + jnp.log(1.0 + jnp.exp(-jnp.abs(xs)))
    g = -jnp.exp(a_ref[...]) * softplus
    pos = row & (DN_C - 1)
    d = 1
    while d < DN_C:
        g = g + jnp.where(pos >= d, pltpu.roll(g, d, axis=0), 0.0)
        d *= 2
    o_ref[...] = jnp.where(lane < DN_V_HEADS, _sigmoid(x), g)


def _dn_gates(bd, a_log, dt_bias, tm=512):
    t = bd.shape[0]
    tm = min(tm, t)
    zeros = jnp.zeros((DN_V_HEADS,), F32)
    a2 = jnp.concatenate([zeros, a_log]).reshape(1, -1)
    b2 = jnp.concatenate([zeros, dt_bias]).reshape(1, -1)
    w = 2 * DN_V_HEADS
    return pl.pallas_call(
        _dn_gate_kernel,
        grid=(t // tm,),
        in_specs=[pl.BlockSpec((tm, w), lambda i: (i, 0)),
                  pl.BlockSpec((1, w), lambda i: (0, 0)),
                  pl.BlockSpec((1, w), lambda i: (0, 0))],
        out_specs=pl.BlockSpec((tm, w), lambda i: (i, 0)),
        out_shape=jax.ShapeDtypeStruct((t, w), F32),
        compiler_params=_cparams(("arbitrary",)),
    )(bd, a2, b2)


def _bdot(a, b):
    return jnp.einsum('bij,bjk->bik', a.astype(BF16), b.astype(BF16), preferred_element_type=F32)


def _unit_lower_inverse(a, ii, jj):
    c = a.shape[-1]
    eye = (ii == jj).astype(F32)
    p = jnp.where((ii >> 3) == (jj >> 3), -a, 0.0)
    p2 = _bdot(p, p)
    t = eye + p
    t = t + _bdot(t, p2)
    p4 = _bdot(p2, p2)
    t = t + _bdot(t, p4)
    s = 8
    sh = 3
    while s < c:
        bi = ii >> sh
        bj = jj >> sh
        off = jnp.where(((bi & 1) == 1) & (bj == bi - 1), a, 0.0)
        t = t - _bdot(_bdot(t, off), t)
        s *= 2
        sh += 1
    return t


def _delta_kernel(q_ref, k_ref, v_ref, z_ref, bg_ref, gr_ref, nw_ref, o_ref, s_ref, *, nchunk, nh):
    pp = pl.program_id(0)
    i = pl.program_id(1)

    @pl.when(i == 0)
    def _():
        s_ref[...] = jnp.zeros_like(s_ref)

    c = DN_C
    hd = DN_HD
    bg = bg_ref[...]
    lane = lax.broadcasted_iota(jnp.int32, bg.shape, 1)
    ii = lax.broadcasted_iota(jnp.int32, (c, c), 0)
    jj = lax.broadcasted_iota(jnp.int32, (c, c), 1)
    nw = nw_ref[...]

    q3 = q_ref[...].reshape(nchunk, c, hd)
    k3 = k_ref[...].reshape(nchunk, c, hd)
    kk = jnp.einsum('cid,cjd->cij', k3, k3, preferred_element_type=F32)
    qk = jnp.einsum('cid,cjd->cij', q3, k3, preferred_element_type=F32)
    kf = k3.astype(F32)
    qf = q3.astype(F32)
    a_l, attn_l, rhs_l, qd_l, kd_l, gl_l = [], [], [], [], [], []
    for j in range(nh):
        hh = nh * pp + j
        beta = jnp.sum(jnp.where(lane == hh, bg, 0.0), axis=1, keepdims=True).reshape(nchunk, c, 1)
        gc = jnp.sum(jnp.where(lane == hh + DN_V_HEADS, bg, 0.0), axis=1,
                     keepdims=True).reshape(nchunk, c, 1)
        gr = gr_ref[j]
        decay = jnp.exp(jnp.where(ii >= jj, gc - gr, -jnp.inf))
        a_l.append(jnp.where(ii > jj, kk * decay, 0.0) * beta)
        attn_l.append((qk * decay).astype(BF16))
        eg = jnp.exp(gc)
        v = v_ref[:, j * hd:(j + 1) * hd].astype(F32).reshape(nchunk, c, hd)
        rhs_l.append(jnp.concatenate([v * beta, kf * (beta * eg)], axis=-1))
        qd_l.append((qf * eg).astype(BF16))
        g_last = gr[:, :, c - 1:c]
        kd_l.append(kf * jnp.exp(g_last - gc))
        gl_l.append(jnp.exp(g_last))
    tinv = _unit_lower_inverse(jnp.concatenate(a_l, axis=0), ii, jj)
    sol = _bdot(tinv, jnp.concatenate(rhs_l, axis=0))

    s = [s_ref[j] for j in range(nh)]
    for ci in range(nchunk):
        r0 = ci * c
        for j in range(nh):
            b = j * nchunk + ci
            u_c = sol[b, :, :hd]
            w_c = sol[b, :, hd:]
            sb = s[j].astype(BF16)
            v_new = u_c - jnp.dot(w_c.astype(BF16), sb, preferred_element_type=F32)
            vb = v_new.astype(BF16)
            o = (jnp.dot(qd_l[j][ci], sb, preferred_element_type=F32)
                 + jnp.dot(attn_l[j][ci], vb, preferred_element_type=F32))
            s[j] = s[j] * gl_l[j][ci] + jnp.dot(kd_l[j][ci].T.astype(BF16), vb,
                                               preferred_element_type=F32)
            ms = jnp.mean(o * o, axis=-1, keepdims=True)
            z = z_ref[r0:r0 + c, j * hd:(j + 1) * hd].astype(F32)
            o = o * lax.rsqrt(ms + RMS_EPS) * nw * (z * _sigmoid(z))
            o_ref[r0:r0 + c, j * hd:(j + 1) * hd] = o.astype(o_ref.dtype)
    for j in range(nh):
        s_ref[j] = s[j]


def _delta_rule(qkv, proj, bg, gc_rows, norm_w, rb=1024):
    t = qkv.shape[0]
    rb = min(rb, t)
    nchunk = rb // DN_C
    nh = DN_V_HEADS // DN_QK_HEADS
    kq = DN_KEY_W // DN_HD
    voff = 2 * DN_KEY_W // (nh * DN_HD)
    zoff = DN_CONV_CH // (nh * DN_HD)
    return pl.pallas_call(
        functools.partial(_delta_kernel, nchunk=nchunk, nh=nh),
        grid=(DN_QK_HEADS, t // rb),
        in_specs=[pl.BlockSpec((rb, DN_HD), lambda p, i: (i, p)),
                  pl.BlockSpec((rb, DN_HD), lambda p, i: (i, kq + p)),
                  pl.BlockSpec((rb, nh * DN_HD), lambda p, i: (i, voff + p)),
                  pl.BlockSpec((rb, nh * DN_HD), lambda p, i: (i, zoff + p)),
                  pl.BlockSpec((rb, 2 * DN_V_HEADS), lambda p, i: (i, 0)),
                  pl.BlockSpec((nh, nchunk, 1, DN_C), lambda p, i: (p, i, 0, 0)),
                  pl.BlockSpec((1, DN_HD), lambda p, i: (0, 0))],
        out_specs=pl.BlockSpec((rb, nh * DN_HD), lambda p, i: (i, p)),
        out_shape=jax.ShapeDtypeStruct((t, DN_VAL_W), BF16),
        scratch_shapes=[pltpu.VMEM((nh, DN_HD, DN_HD), F32)],
        compiler_params=_cparams(("arbitrary", "arbitrary")),
    )(qkv, qkv, qkv, proj, bg, gc_rows, norm_w.reshape(1, DN_HD))


def _router_kernel(h_ref, w_ref, b_ref, idx_ref, wt_ref):
    logits = _dot3(h_ref[...], w_ref[...]) + b_ref[...]
    tm, e = logits.shape
    lane = lax.broadcasted_iota(jnp.int32, (tm, e), 1).astype(F32)
    lane_o = lax.broadcasted_iota(jnp.int32, (tm, LANE), 1)
    idx_out = jnp.zeros((tm, LANE), jnp.int32)
    val_out = jnp.zeros((tm, LANE), F32)
    cur = logits
    vals = []
    for kth in range(TOP_K):
        m = jnp.max(cur, axis=-1, keepdims=True)
        sel = jnp.min(jnp.where(cur == m, lane, float(e)), axis=-1, keepdims=True)
        cur = jnp.where(lane == sel, -jnp.inf, cur)
        idx_out = jnp.where(lane_o == kth, sel.astype(jnp.int32), idx_out)
        vals.append(m)
    es = [jnp.exp(v - vals[0]) for v in vals]
    tot = es[0]
    for x in es[1:]:
        tot = tot + x
    for kth in range(TOP_K):
        val_out = jnp.where(lane_o == kth, es[kth] / tot, val_out)
    idx_ref[...] = idx_out
    wt_ref[...] = val_out


def _router(h, w_router, b_router, tm=512):
    t, d = h.shape
    tm = min(tm, t)
    e = w_router.shape[1]
    return pl.pallas_call(
        _router_kernel,
        grid=(t // tm,),
        in_specs=[pl.BlockSpec((tm, d), lambda i: (i, 0)),
                  pl.BlockSpec((d, e), lambda i: (0, 0)),
                  pl.BlockSpec((1, e), lambda i: (0, 0))],
        out_specs=[pl.BlockSpec((tm, LANE), lambda i: (i, 0)),
                   pl.BlockSpec((tm, LANE), lambda i: (i, 0))],
        out_shape=[jax.ShapeDtypeStruct((t, LANE), jnp.int32),
                   jax.ShapeDtypeStruct((t, LANE), F32)],
        compiler_params=_cparams(("arbitrary",)),
    )(h, w_router, b_router.reshape(1, e))


def _expert_kernel(be_ref, na_ref, tok_ref, h_ref, wg_ref, wu_ref, wd_ref, bg_ref, bu_ref, bd_ref,
                   o_ref, wgb_ref, wub_ref, wdb_ref, xbuf, sem):
    i = pl.program_id(0)
    na = na_ref[0]
    bm = xbuf.shape[1]

    def row_copy(blk, slot, r):
        tok = tok_ref[blk * bm + r]
        return pltpu.make_async_copy(h_ref.at[pl.ds(tok, 1)], xbuf.at[slot, pl.ds(r, 1)], sem.at[slot])

    def start_block(blk, slot):
        def body(r, carry):
            row_copy(blk, slot, r).start()
            return carry
        lax.fori_loop(0, bm, body, 0, unroll=8)

    @pl.when(i == 0)
    def _():
        start_block(0, 0)

    @pl.when(i + 1 < na)
    def _():
        start_block(i + 1, (i + 1) % 2)

    prev = be_ref[jnp.maximum(i - 1, 0)]
    changed = jnp.logical_or(i == 0, be_ref[i] != prev)

    @pl.when(changed)
    def _():
        wgb_ref[...] = wg_ref[...].astype(BF16)
        wub_ref[...] = wu_ref[...].astype(BF16)
        wdb_ref[...] = wd_ref[...].astype(BF16)

    @pl.when(i < na)
    def _():
        slot = i % 2

        def wait_body(r, carry):
            row_copy(i, slot, r).wait()
            return carry
        lax.fori_loop(0, bm, wait_body, 0, unroll=8)
        x = xbuf[slot].astype(BF16)
        gate = jnp.minimum(jnp.dot(x, wgb_ref[...], preferred_element_type=F32) + bg_ref[...],
                           SWIGLU_LIMIT)
        up = jnp.clip(jnp.dot(x, wub_ref[...], preferred_element_type=F32) + bu_ref[...],
                      -SWIGLU_LIMIT, SWIGLU_LIMIT)
        hid = gate * _sigmoid(SWIGLU_ALPHA * gate) * (up + 1.0)
        y = jnp.dot(hid.astype(BF16), wdb_ref[...], preferred_element_type=F32) + bd_ref[...]
        o_ref[...] = y.astype(o_ref.dtype)

    @pl.when(i >= na)
    def _():
        o_ref[...] = jnp.zeros_like(o_ref)


def _experts(h, slot_tok, block_e, n_active, layer, w_gate, b_gate, w_up, b_up, w_down, b_down):
    d = h.shape[1]
    n_slots = slot_tok.shape[0]
    nb = n_slots // MOE_BM
    f = w_gate.shape[-1]
    l = layer

    def wspec(shape):
        return pl.BlockSpec((None, None) + shape, lambda i, be, na, tok: (l, be[i], 0, 0))

    grid_spec = pltpu.PrefetchScalarGridSpec(
        num_scalar_prefetch=3,
        grid=(nb,),
        in_specs=[pl.BlockSpec(memory_space=pl.ANY),
                  wspec((d, f)), wspec((d, f)), wspec((f, d)),
                  wspec((1, f)), wspec((1, f)), wspec((1, d))],
        out_specs=pl.BlockSpec((MOE_BM, d), lambda i, be, na, tok: (i, 0)),
        scratch_shapes=[pltpu.VMEM((d, f), BF16), pltpu.VMEM((d, f), BF16), pltpu.VMEM((f, d), BF16),
                        pltpu.VMEM((2, MOE_BM, d), F32), pltpu.SemaphoreType.DMA((2,))],
    )
    nl, ne = b_gate.shape[:2]
    return pl.pallas_call(
        _expert_kernel,
        grid_spec=grid_spec,
        out_shape=jax.ShapeDtypeStruct((n_slots, d), F32),
        compiler_params=_cparams(("arbitrary",)),
    )(block_e, n_active, slot_tok, h, w_gate, w_up, w_down,
      b_gate.reshape(nl, ne, 1, f), b_up.reshape(nl, ne, 1, f), b_down.reshape(nl, ne, 1, d))


def _routing(top_idx):
    t = top_idx.shape[0]
    sel = jnp.sum(jax.nn.one_hot(top_idx, N_EXPERTS, dtype=jnp.int32), axis=1)
    counts = jnp.sum(sel, axis=0)
    before = jnp.cumsum(sel, axis=0) - sel
    padded = (counts + MOE_BM - 1) // MOE_BM * MOE_BM
    pend = jnp.cumsum(padded)
    pstart = pend - padded
    dest = pstart[top_idx] + jnp.take_along_axis(before, top_idx, axis=1)
    n_blocks = t * TOP_K // MOE_BM + N_EXPERTS
    n_active = (pend[-1] // MOE_BM).astype(jnp.int32)
    blk = jnp.arange(n_blocks, dtype=jnp.int32)
    blk = jnp.minimum(blk, jnp.maximum(n_active - 1, 0))
    block_e = jnp.sum((blk[:, None] * MOE_BM >= pend[None, :]).astype(jnp.int32), axis=1)
    block_e = jnp.minimum(block_e, N_EXPERTS - 1).astype(jnp.int32)
    tok = jnp.broadcast_to(jnp.arange(t, dtype=jnp.int32)[:, None], dest.shape)
    slot_tok = jnp.zeros((n_blocks * MOE_BM,), jnp.int32).at[dest.reshape(-1)].set(tok.reshape(-1))
    return dest.astype(jnp.int32), slot_tok, block_e, n_active.reshape(1)


def _combine_kernel(dest_ref, h_ref, w_ref, g_ref, b_ref, ys_ref, o_ref, obf_ref, buf, sem, *, tm):
    i = pl.program_id(0)
    n = pl.num_programs(0)

    def row_copy(blk, slot, r, k):
        d = dest_ref[(blk * tm + r) * TOP_K + k]
        return pltpu.make_async_copy(ys_ref.at[pl.ds(d, 1)], buf.at[slot, k, pl.ds(r, 1)],
                                     sem.at[slot])

    def start_block(blk, slot):
        def body(r, carry):
            for k in range(TOP_K):
                row_copy(blk, slot, r, k).start()
            return carry
        lax.fori_loop(0, tm, body, 0)

    def wait_block(blk, slot):
        def body(r, carry):
            for k in range(TOP_K):
                row_copy(blk, slot, r, k).wait()
            return carry
        lax.fori_loop(0, tm, body, 0)

    @pl.when(i == 0)
    def _():
        start_block(0, 0)

    @pl.when(i + 1 < n)
    def _():
        start_block(i + 1, (i + 1) % 2)

    slot = i % 2
    wait_block(i, slot)
    w = w_ref[...]
    ffn = w[:, 0:1] * buf[slot, 0]
    for k in range(1, TOP_K):
        ffn = ffn + w[:, k:k + 1] * buf[slot, k]
    y = _layer_norm_rows(ALPHA * h_ref[...] + ffn, g_ref[...], b_ref[...])
    o_ref[...] = y
    obf_ref[...] = y.astype(BF16)


def _combine_norm(dest, h, top_w, ys, g, b, tm=128):
    t, d = h.shape
    tm = min(tm, t)
    row = pl.BlockSpec((tm, d), lambda i, dref: (i, 0))
    vec = pl.BlockSpec((1, d), lambda i, dref: (0, 0))
    grid_spec = pltpu.PrefetchScalarGridSpec(
        num_scalar_prefetch=1,
        grid=(t // tm,),
        in_specs=[row, pl.BlockSpec((tm, LANE), lambda i, dref: (i, 0)), vec, vec,
                  pl.BlockSpec(memory_space=pl.ANY)],
        out_specs=[row, row],
        scratch_shapes=[pltpu.VMEM((2, TOP_K, tm, d), F32), pltpu.SemaphoreType.DMA((2,))],
    )
    return pl.pallas_call(
        functools.partial(_combine_kernel, tm=tm),
        grid_spec=grid_spec,
        out_shape=[jax.ShapeDtypeStruct((t, d), F32), jax.ShapeDtypeStruct((t, d), BF16)],
        compiler_params=_cparams(("arbitrary",)),
    )(dest.reshape(-1), h, top_w, g.reshape(1, d), b.reshape(1, d), ys)


def _layer(l, h, hb, p):
    t = h.shape[0]
    w_in = p['w_in']
    u = _matmul(hb, [(w_in, (l,), 0)], D_MODEL, F32, lambda acc: acc, tn=1024)
    proj = _matmul(hb, [(w_in, (l,), OFF_QKV // 1024)], OFF_BETA - OFF_QKV, BF16, lambda acc: acc,
                   tn=1024)
    tail = lax.optimization_barrier(w_in[l, :, OFF_BETA:])
    gates = _matmul(hb, [(tail[:, 2 * DN_V_HEADS:], (), 0)], 2 * D_MODEL, BF16,
                    lambda acc: _sigmoid(acc), tn=1024)
    bd = _matmul(hb, [(tail[:, :2 * DN_V_HEADS], (), 0)], 2 * DN_V_HEADS, F32,
                 lambda acc: acc, tn=2 * DN_V_HEADS)

    tables = _s5_tables(p['s5_lam_re'][l], p['s5_lam_im'][l], p['s5_log_dt'][l], p['s5_b_re'][l],
                        p['s5_b_im'][l], p['s5_c_re'][l], p['s5_c_im'][l], p['s5_d'][l], t // S5_L)
    y = _s5_apply(u, tables)
    part = _matmul(y, [(p['w_glu_a'], (l,), 0), (p['w_glu_b'], (l,), 0)], D_MODEL, BF16,
                   lambda a, b, g: a * _sigmoid(b) * g.astype(F32), extras=[(gates, 0)])

    qkv = _dn_conv(proj, p['dn_conv_w'][l].T)
    bg = _dn_gates(bd, p['dn_a_log'][l], p['dn_dt_bias'][l])
    gc_rows = bg[:, DN_V_HEADS:].T.reshape(DN_V_HEADS, t // DN_C, 1, DN_C)
    o = _delta_rule(qkv, proj, bg, gc_rows, p['dn_norm_w'][l])
    merged = _matmul(o, [(p['w_dn_out'], (l,), 0)], D_MODEL, BF16,
                     lambda acc, g, s: acc * g.astype(F32) + s.astype(F32),
                     extras=[(gates, D_MODEL // 512), (part, 0)])
    mix = _matmul(merged, [(p['w_mix_out'], (l,), 0)], D_MODEL, F32, lambda acc: acc, tn=1024)
    h, hb = _deepnorm(h, mix, p['ln1_g'][l], p['ln1_b'][l])

    top_idx, top_w = _router(h, p['w_router'][l], p['b_router'][l])
    dest, slot_tok, block_e, n_active = _routing(top_idx[:, :TOP_K])
    ys = _experts(h, slot_tok, block_e, n_active, l, p['w_gate'], p['b_gate'], p['w_up'], p['b_up'],
                  p['w_down'], p['b_down'])
    return _combine_norm(dest, h, top_w, ys, p['ln2_g'][l], p['ln2_b'][l])


def kernel(x, w_in, dn_conv_w, dn_a_log, dn_dt_bias, dn_norm_w, w_dn_out, s5_lam_re, s5_lam_im, s5_log_dt, s5_b_re, s5_b_im, s5_c_re, s5_c_im, s5_d, w_glu_a, w_glu_b, w_mix_out, ln1_g, ln1_b, w_router, b_router, w_gate, b_gate, w_up, b_up, w_down, b_down, ln2_g, ln2_b):
    p = dict(w_in=w_in, dn_conv_w=dn_conv_w, dn_a_log=dn_a_log, dn_dt_bias=dn_dt_bias,
             dn_norm_w=dn_norm_w, w_dn_out=w_dn_out, s5_lam_re=s5_lam_re, s5_lam_im=s5_lam_im,
             s5_log_dt=s5_log_dt, s5_b_re=s5_b_re, s5_b_im=s5_b_im, s5_c_re=s5_c_re,
             s5_c_im=s5_c_im, s5_d=s5_d, w_glu_a=w_glu_a, w_glu_b=w_glu_b, w_mix_out=w_mix_out,
             ln1_g=ln1_g, ln1_b=ln1_b, w_router=w_router, b_router=b_router, w_gate=w_gate,
             b_gate=b_gate, w_up=w_up, b_up=b_up, w_down=w_down, b_down=b_down,
             ln2_g=ln2_g, ln2_b=ln2_b)
    bsz, t, d = x.shape
    h = x.reshape(bsz * t, d)
    hb = h.astype(BF16)
    for l in range(w_in.shape[0]):
        h, hb = _layer(l, h, hb, p)
    return h.reshape(bsz, t, d)
```

```python
import functools
import math

import jax
import jax.numpy as jnp
from jax import lax
from jax.experimental import pallas as pl
from jax.experimental.pallas import tpu as pltpu

F32 = jnp.float32
BF16 = jnp.bfloat16

D_MODEL = 2048
DEPTH = 4
S5_P = 16
S5_N = 64
S5_G = D_MODEL // S5_P
S5_L = 16
DN_QK_HEADS = 16
DN_V_HEADS = 32
DN_HD = 128
DN_KEY_W = DN_QK_HEADS * DN_HD
DN_VAL_W = DN_V_HEADS * DN_HD
DN_CONV_CH = 2 * DN_KEY_W + DN_VAL_W
DN_CONV = 4
DN_C = 128
N_EXPERTS = 32
TOP_K = 4
D_EXPERT = 512
SWIGLU_LIMIT = 7.0
SWIGLU_ALPHA = 1.702
MOE_BM = 256
ALPHA = (2 * DEPTH) ** 0.25
LN_EPS = 1e-5
RMS_EPS = 1e-6
L2_EPS = 1e-6
S5_MAX_REAL = -1e-4
OFF_QKV = D_MODEL
OFF_Z = OFF_QKV + DN_CONV_CH
OFF_BETA = OFF_Z + DN_VAL_W
OFF_GATE_S5 = OFF_BETA + 2 * DN_V_HEADS
IN_COLS = OFF_GATE_S5 + 2 * D_MODEL
LANE = 128
VMEM_LIMIT = 56 * 1024 * 1024


def _cparams(sem):
    return pltpu.CompilerParams(dimension_semantics=sem, vmem_limit_bytes=VMEM_LIMIT)


_NT = (((1,), (1,)), ((), ()))
_NN = (((1,), (0,)), ((), ()))


def _mm_kernel(*refs, nw, nx, epilogue, dims):
    a_ref = refs[0]
    w_refs = refs[1:1 + nw]
    x_refs = refs[1 + nw:1 + nw + nx]
    o_ref = refs[1 + nw + nx]
    wbf_refs = refs[2 + nw + nx:]

    @pl.when(pl.program_id(1) == 0)
    def _():
        for w_ref, wbf_ref in zip(w_refs, wbf_refs):
            wbf_ref[...] = w_ref[...].astype(BF16)

    a = a_ref[...].astype(BF16)
    accs = [lax.dot_general(a, wbf[...], dims, preferred_element_type=F32) for wbf in wbf_refs]
    o_ref[...] = epilogue(*accs, *[x[...] for x in x_refs]).astype(o_ref.dtype)


def _matmul(a, ws, n_out, out_dtype, epilogue, extras=(), tm=512, tn=512, w_rows_are_outputs=False):
    m, k = a.shape
    tm = min(tm, m)
    grid = (n_out // tn, m // tm)
    in_specs = [pl.BlockSpec((tm, k), lambda j, i: (i, 0))]
    args = [a]
    wshape = (tn, k) if w_rows_are_outputs else (k, tn)
    for w, lead, off in ws:
        nlead = len(lead)
        if w_rows_are_outputs:
            imap = lambda j, i, lead=lead, off=off: tuple(lead) + (j + off, 0)
        else:
            imap = lambda j, i, lead=lead, off=off: tuple(lead) + (0, j + off)
        in_specs.append(pl.BlockSpec((None,) * nlead + wshape, imap))
        args.append(w)
    for x, off in extras:
        in_specs.append(pl.BlockSpec((tm, tn), lambda j, i, off=off: (i, j + off)))
        args.append(x)
    return pl.pallas_call(
        functools.partial(_mm_kernel, nw=len(ws), nx=len(extras), epilogue=epilogue,
                          dims=_NT if w_rows_are_outputs else _NN),
        grid=grid,
        in_specs=in_specs,
        out_specs=pl.BlockSpec((tm, tn), lambda j, i: (i, j)),
        out_shape=jax.ShapeDtypeStruct((m, n_out), out_dtype),
        scratch_shapes=[pltpu.VMEM(wshape, BF16) for _ in ws],
        compiler_params=_cparams(("arbitrary", "arbitrary")),
    )(*args)


def _sigmoid(x):
    return 1.0 / (1.0 + jnp.exp(-x))


def _split_bf16(x):
    hi = x.astype(BF16)
    lo = (x - hi.astype(F32)).astype(BF16)
    return hi, lo


def _dot3(a, b, dims=_NN):
    ah, al = _split_bf16(a)
    bh, bl = _split_bf16(b)
    return (lax.dot_general(ah, bh, dims, preferred_element_type=F32)
            + lax.dot_general(ah, bl, dims, preferred_element_type=F32)
            + lax.dot_general(al, bh, dims, preferred_element_type=F32))


ROW_TILES = D_MODEL // LANE


def _store_token_rows(ref, y):
    n = y.shape[0]
    for j in range(ROW_TILES):
        ref[pl.ds(j, n, stride=ROW_TILES), :] = y[:, j * LANE:(j + 1) * LANE]


def _load_token_rows(ref, n):
    return jnp.concatenate([ref[pl.ds(j, n, stride=ROW_TILES), :] for j in range(ROW_TILES)], axis=1)


def _layer_norm_rows(x, g, b):
    mu = jnp.mean(x, axis=-1, keepdims=True)
    xc = x - mu
    var = jnp.mean(xc * xc, axis=-1, keepdims=True)
    return xc * lax.rsqrt(var + LN_EPS) * g + b


def _ln_kernel(h_ref, r_ref, g_ref, b_ref, o_ref, obf_ref, orow_ref):
    y = _layer_norm_rows(ALPHA * h_ref[...] + r_ref[...], g_ref[...], b_ref[...])
    o_ref[...] = y
    obf_ref[...] = y.astype(BF16)
    _store_token_rows(orow_ref, y)


def _deepnorm(h, r, g, b, tm=256):
    t, d = h.shape
    tm = min(tm, t)
    row = pl.BlockSpec((tm, d), lambda i: (i, 0))
    vec = pl.BlockSpec((1, d), lambda i: (0, 0))
    return pl.pallas_call(
        _ln_kernel,
        grid=(t // tm,),
        in_specs=[row, row, vec, vec],
        out_specs=[row, row, pl.BlockSpec((tm * ROW_TILES, LANE), lambda i: (i, 0))],
        out_shape=[jax.ShapeDtypeStruct((t, d), F32), jax.ShapeDtypeStruct((t, d), BF16),
                   jax.ShapeDtypeStruct((t * ROW_TILES, LANE), F32)],
        compiler_params=_cparams(("arbitrary",)),
    )(h, r, g.reshape(1, d), b.reshape(1, d))


def _s5_tables(lam_re, lam_im, log_dt, b_re, b_im, c_re, c_im, d_skip, n_chunks):
    L = S5_L
    g = lam_re.shape[0]
    lre = jnp.minimum(lam_re, S5_MAX_REAL)
    lim = lam_im
    dt = jnp.exp(log_dt)[:, None]
    ks = jnp.arange(L + 1, dtype=F32)[:, None, None]
    mag = jnp.exp(lre * dt * ks)
    pr = mag * jnp.cos(lim * dt * ks)
    pi = mag * jnp.sin(lim * dt * ks)
    a_re, a_im = pr[1], pi[1]
    den = lre * lre + lim * lim
    f_re = ((a_re - 1.0) * lre + a_im * lim) / den
    f_im = (a_im * lre - (a_re - 1.0) * lim) / den
    bb_re = f_re[..., None] * b_re - f_im[..., None] * b_im
    bb_im = f_re[..., None] * b_im + f_im[..., None] * b_re
    bb = jnp.concatenate([jnp.transpose(bb_re, (0, 2, 1)), jnp.transpose(bb_im, (0, 2, 1))], axis=-1)
    cc = jnp.concatenate([c_re, c_im], axis=-1)
    qr = jnp.transpose(pr[L - 1::-1], (1, 0, 2))
    qi = jnp.transpose(pi[L - 1::-1], (1, 0, 2))
    p1 = jnp.concatenate([qr, qr], axis=-1)
    p2 = jnp.concatenate([-qi, qi], axis=-1)
    ur = jnp.transpose(pr[1:], (1, 0, 2))
    ui = jnp.transpose(pi[1:], (1, 0, 2))
    q1 = jnp.concatenate([ur, -ur], axis=-1)
    q2 = jnp.concatenate([-ui, -ui], axis=-1)
    nlev = max(1, int(math.log2(n_chunks)))
    mr, mi = pr[L], pi[L]
    m1, m2 = [], []
    for _ in range(nlev):
        m1.append(jnp.concatenate([mr, mr], axis=-1))
        m2.append(jnp.concatenate([-mi, mi], axis=-1))
        mr, mi = mr * mr - mi * mi, 2.0 * mr * mi
    pad = [jnp.zeros_like(m1[0])] * (16 - nlev)
    pw1 = jnp.stack(m1 + pad, axis=1)
    pw2 = jnp.stack(m2 + pad, axis=1)
    dsk = jnp.tile(d_skip, (1, L)).reshape(g, 1, L * S5_P)
    return p1, p2, q1, q2, bb, cc, pw1, pw2, dsk


def _gelu_tanh(y):
    return 0.5 * y * (1.0 + jnp.tanh(0.7978845608028654 * (y + 0.044715 * y * y * y)))


def _rep_rows(x, n):
    r, w = x.shape
    return jnp.broadcast_to(x[:, None, :], (r, n, w)).reshape(r * n, w)


def _tile_rows(x, n):
    r, w = x.shape
    return jnp.broadcast_to(x[None, :, :], (n, r, w)).reshape(n * r, w)


def _s5_kernel(u_ref, p1_ref, p2_ref, q1_ref, q2_ref, bb_ref, cc_ref, pw1_ref, pw2_ref, d_ref,
               o_ref, perm_ref, *, nlev, gb):
    L, P, N = S5_L, S5_P, S5_N
    lp = L * P
    half = (L // 2) * LANE
    nc = u_ref.shape[0] // L

    @pl.when(pl.program_id(0) == 0)
    def _():
        r = lax.broadcasted_iota(jnp.int32, (half, half), 0)
        c = lax.broadcasted_iota(jnp.int32, (half, half), 1)
        dst = ((r >> 4) & 7) * LANE + (r >> 7) * P + (r & 15)
        perm_ref[...] = jnp.where(c == dst, 1.0, 0.0).astype(BF16)

    perm = perm_ref[...]
    v = []
    for th in range(2):
        xs = [u_ref[pl.ds(th * 8 + tl, nc, stride=L), :].astype(BF16) for tl in range(8)]
        v.append(jnp.dot(jnp.concatenate(xs, axis=1), perm, preferred_element_type=F32).astype(BF16))

    row = lax.broadcasted_iota(jnp.int32, (nc, 2 * N), 0)
    lane_n = lax.broadcasted_iota(jnp.int32, (1, 2 * N), 1)
    sign = jnp.where(lane_n < N, 1.0, -1.0)
    rblk = lax.broadcasted_iota(jnp.int32, (lp, lp), 0) >> 4
    cblk = lax.broadcasted_iota(jnp.int32, (lp, lp), 1) >> 4
    z = [[], []]
    for gi in range(gb):
        ug = jnp.concatenate([v[0][:, gi * LANE:(gi + 1) * LANE],
                              v[1][:, gi * LANE:(gi + 1) * LANE]], axis=1)
        bbg = bb_ref[gi]
        ccg = cc_ref[gi]
        bmat = (_rep_rows(p1_ref[gi], P) * _tile_rows(bbg, L)
                + _rep_rows(p2_ref[gi], P) * _tile_rows(pltpu.roll(bbg, N, axis=1), L))
        cmt = (_rep_rows(q1_ref[gi], P) * _tile_rows(ccg, L)
               + _rep_rows(q2_ref[gi], P) * _tile_rows(pltpu.roll(ccg, N, axis=1), L))
        w = _dot3(bmat, _tile_rows(ccg * sign, L), _NT)
        tmat = jnp.zeros((lp, lp), F32)
        for t in range(L):
            shift = (lp - (L - 1 - t) * P) % lp
            tmat = jnp.where((cblk == t) & (rblk <= t), pltpu.roll(w, shift, axis=0), tmat)
        x = jnp.dot(ug, bmat.astype(BF16), preferred_element_type=F32)
        for lev in range(nlev):
            d = 1 << lev
            m1 = pw1_ref[gi, lev:lev + 1, :]
            m2 = pw2_ref[gi, lev:lev + 1, :]
            sh = jnp.where(row >= d, pltpu.roll(x, d, axis=0), 0.0)
            x = x + m1 * sh + m2 * pltpu.roll(sh, N, axis=1)
        sprev = jnp.where(row >= 1, pltpu.roll(x, 1, axis=0), 0.0)
        y = (jnp.dot(ug, tmat.astype(BF16), preferred_element_type=F32)
             + lax.dot_general(sprev.astype(BF16), cmt.astype(BF16), _NT, preferred_element_type=F32)
             + d_ref[gi] * ug.astype(F32))
        yg = _gelu_tanh(y).astype(BF16)
        z[0].append(yg[:, :LANE])
        z[1].append(yg[:, LANE:])
    for th in range(2):
        yp = lax.dot_general(jnp.concatenate(z[th], axis=1), perm, _NT, preferred_element_type=F32)
        for tl in range(8):
            o_ref[pl.ds(th * 8 + tl, nc, stride=L), :] = yp[:, tl * LANE:(tl + 1) * LANE]


def _s5_apply(u, tables):
    t, d = u.shape
    gb = LANE // S5_P
    nc = t // S5_L
    nlev = int(math.log2(nc))
    assert (1 << nlev) == nc and nlev <= 16
    tab = pl.BlockSpec((gb, 16, 2 * S5_N), lambda i: (i, 0, 0))
    return pl.pallas_call(
        functools.partial(_s5_kernel, nlev=nlev, gb=gb),
        grid=(d // LANE,),
        in_specs=[pl.BlockSpec((t, LANE), lambda i: (0, i))] + [tab] * 8
                 + [pl.BlockSpec((gb, 1, S5_L * S5_P), lambda i: (i, 0, 0))],
        out_specs=pl.BlockSpec((t, LANE), lambda i: (0, i)),
        out_shape=jax.ShapeDtypeStruct((t, d), F32),
        scratch_shapes=[pltpu.VMEM((8 * LANE, 8 * LANE), BF16)],
        compiler_params=_cparams(("arbitrary",)),
    )(u, *tables)


def _conv_kernel(x_ref, p_ref, w_ref, o_ref, xs_ref, *, nq, nqk):
    j = pl.program_id(0)
    i = pl.program_id(1)
    tm, tc = x_ref.shape
    prev = p_ref[...].astype(F32)
    xs_ref[0:8, :] = jnp.where(i > 0, prev[8:16], 0.0)
    xs_ref[8:8 + tm, :] = x_ref[...].astype(F32)
    w = w_ref[...]
    acc = xs_ref[8:8 + tm, :] * w[DN_CONV - 1:DN_CONV, :]
    for s in range(1, DN_CONV):
        acc = acc + xs_ref[8 - s:8 - s + tm, :] * w[DN_CONV - 1 - s:DN_CONV - s, :]
    y = acc * _sigmoid(acc)
    qscale = jnp.where(j < nq, DN_HD ** -0.5, 1.0)
    for hh in range(tc // DN_HD):
        blk = y[:, hh * DN_HD:(hh + 1) * DN_HD]
        ss = jnp.sum(blk * blk, axis=-1, keepdims=True)
        fac = jnp.where(j < nqk, lax.rsqrt(ss + L2_EPS) * qscale, 1.0)
        o_ref[:, hh * DN_HD:(hh + 1) * DN_HD] = (blk * fac).astype(o_ref.dtype)


def _dn_conv(proj, conv_wt, tm=512, tc=512):
    t = proj.shape[0]
    tm = min(tm, t)
    pb = tm // 16
    return pl.pallas_call(
        functools.partial(_conv_kernel, nq=DN_KEY_W // tc, nqk=2 * DN_KEY_W // tc),
        grid=(DN_CONV_CH // tc, t // tm),
        in_specs=[pl.BlockSpec((tm, tc), lambda j, i: (i, j)),
                  pl.BlockSpec((16, tc), lambda j, i: (jnp.maximum(i * pb - 1, 0), j)),
                  pl.BlockSpec((DN_CONV, tc), lambda j, i: (0, j))],
        out_specs=pl.BlockSpec((tm, tc), lambda j, i: (i, j)),
        out_shape=jax.ShapeDtypeStruct((t, DN_CONV_CH), BF16),
        scratch_shapes=[pltpu.VMEM((tm + 8, tc), F32)],
        compiler_params=_cparams(("arbitrary", "arbitrary")),
    )(proj, proj, conv_wt)


def _dn_gate_kernel(x_ref, a_ref, dtb_ref, o_ref):
    x = x_ref[...]
    tm, w = x.shape
    lane = lax.broadcasted_iota(jnp.int32, (tm, w), 1)
    row = lax.broadcasted_iota(jnp.int32, (tm, w), 0)
    xs = x + dtb_ref[...]
    softplus = jnp.maximum(xs, 0.0) + jnp.log(1.0 + jnp.exp(-jnp.abs(xs)))
    g = -jnp.exp(a_ref[...]) * softplus
    pos = row & (DN_C - 1)
    d = 1
    while d < DN_C:
        g = g + jnp.where(pos >= d, pltpu.roll(g, d, axis=0), 0.0)
        d *= 2
    o_ref[...] = jnp.where(lane < DN_V_HEADS, _sigmoid(x), g)


def _dn_gates(bd, a_log, dt_bias, tm=512):
    t = bd.shape[0]
    tm = min(tm, t)
    zeros = jnp.zeros((DN_V_HEADS,), F32)
    a2 = jnp.concatenate([zeros, a_log]).reshape(1, -1)
    b2 = jnp.concatenate([zeros, dt_bias]).reshape(1, -1)
    w = 2 * DN_V_HEADS
    return pl.pallas_call(
        _dn_gate_kernel,
        grid=(t // tm,),
        in_specs=[pl.BlockSpec((tm, w), lambda i: (i, 0)),
                  pl.BlockSpec((1, w), lambda i: (0, 0)),
                  pl.BlockSpec((1, w), lambda i: (0, 0))],
        out_specs=pl.BlockSpec((tm, w), lambda i: (i, 0)),
        out_shape=jax.ShapeDtypeStruct((t, w), F32),
        compiler_params=_cparams(("arbitrary",)),
    )(bd, a2, b2)


def _bdot(a, b):
    return jnp.einsum('bij,bjk->bik', a.astype(BF16), b.astype(BF16), preferred_element_type=F32)


def _unit_lower_inverse(a, ii, jj):
    c = a.shape[-1]
    eye = (ii == jj).astype(F32)
    p = jnp.where((ii >> 3) == (jj >> 3), -a, 0.0)
    p2 = _bdot(p, p)
    t = eye + p
    t = t + _bdot(t, p2)
    p4 = _bdot(p2, p2)
    t = t + _bdot(t, p4)
    s = 8
    sh = 3
    while s < c:
        bi = ii >> sh
        bj = jj >> sh
        off = jnp.where(((bi & 1) == 1) & (bj == bi - 1), a, 0.0)
        t = t - _bdot(_bdot(t, off), t)
        s *= 2
        sh += 1
    return t


def _delta_kernel(q_ref, k_ref, v_ref, z_ref, bg_ref, gr_ref, nw_ref, o_ref, s_ref, *, nchunk, nh):
    pp = pl.program_id(0)
    i = pl.program_id(1)

    @pl.when(i == 0)
    def _():
        s_ref[...] = jnp.zeros_like(s_ref)

    c = DN_C
    hd = DN_HD
    bg = bg_ref[...]
    lane = lax.broadcasted_iota(jnp.int32, bg.shape, 1)
    ii = lax.broadcasted_iota(jnp.int32, (c, c), 0)
    jj = lax.broadcasted_iota(jnp.int32, (c, c), 1)
    nw = nw_ref[...]

    q3 = q_ref[...].reshape(nchunk, c, hd)
    k3 = k_ref[...].reshape(nchunk, c, hd)
    kk = jnp.einsum('cid,cjd->cij', k3, k3, preferred_element_type=F32)
    qk = jnp.einsum('cid,cjd->cij', q3, k3, preferred_element_type=F32)
    kf = k3.astype(F32)
    qf = q3.astype(F32)
    a_l, attn_l, rhs_l, qd_l, kd_l, gl_l = [], [], [], [], [], []
    for j in range(nh):
        hh = nh * pp + j
        beta = jnp.sum(jnp.where(lane == hh, bg, 0.0), axis=1, keepdims=True).reshape(nchunk, c, 1)
        gc = jnp.sum(jnp.where(lane == hh + DN_V_HEADS, bg, 0.0), axis=1,
                     keepdims=True).reshape(nchunk, c, 1)
        gr = gr_ref[j]
        decay = jnp.exp(jnp.where(ii >= jj, gc - gr, -jnp.inf))
        a_l.append(jnp.where(ii > jj, kk * decay, 0.0) * beta)
        attn_l.append((qk * decay).astype(BF16))
        eg = jnp.exp(gc)
        v = v_ref[:, j * hd:(j + 1) * hd].astype(F32).reshape(nchunk, c, hd)
        rhs_l.append(jnp.concatenate([v * beta, kf * (beta * eg)], axis=-1))
        qd_l.append((qf * eg).astype(BF16))
        g_last = gr[:, :, c - 1:c]
        kd_l.append(kf * jnp.exp(g_last - gc))
        gl_l.append(jnp.exp(g_last))
    tinv = _unit_lower_inverse(jnp.concatenate(a_l, axis=0), ii, jj)
    sol = _bdot(tinv, jnp.concatenate(rhs_l, axis=0))

    s = [s_ref[j] for j in range(nh)]
    for ci in range(nchunk):
        r0 = ci * c
        for j in range(nh):
            b = j * nchunk + ci
            u_c = sol[b, :, :hd]
            w_c = sol[b, :, hd:]
            sb = s[j].astype(BF16)
            v_new = u_c - jnp.dot(w_c.astype(BF16), sb, preferred_element_type=F32)
            vb = v_new.astype(BF16)
            o = (jnp.dot(qd_l[j][ci], sb, preferred_element_type=F32)
                 + jnp.dot(attn_l[j][ci], vb, preferred_element_type=F32))
            s[j] = s[j] * gl_l[j][ci] + jnp.dot(kd_l[j][ci].T.astype(BF16), vb,
                                               preferred_element_type=F32)
            ms = jnp.mean(o * o, axis=-1, keepdims=True)
            z = z_ref[r0:r0 + c, j * hd:(j + 1) * hd].astype(F32)
            o = o * lax.rsqrt(ms + RMS_EPS) * nw * (z * _sigmoid(z))
            o_ref[r0:r0 + c, j * hd:(j + 1) * hd] = o.astype(o_ref.dtype)
    for j in range(nh):
        s_ref[j] = s[j]


def _delta_rule(qkv, proj, bg, gc_rows, norm_w, rb=1024):
    t = qkv.shape[0]
    rb = min(rb, t)
    nchunk = rb // DN_C
    nh = DN_V_HEADS // DN_QK_HEADS
    kq = DN_KEY_W // DN_HD
    voff = 2 * DN_KEY_W // (nh * DN_HD)
    zoff = DN_CONV_CH // (nh * DN_HD)
    return pl.pallas_call(
        functools.partial(_delta_kernel, nchunk=nchunk, nh=nh),
        grid=(DN_QK_HEADS, t // rb),
        in_specs=[pl.BlockSpec((rb, DN_HD), lambda p, i: (i, p)),
                  pl.BlockSpec((rb, DN_HD), lambda p, i: (i, kq + p)),
                  pl.BlockSpec((rb, nh * DN_HD), lambda p, i: (i, voff + p)),
                  pl.BlockSpec((rb, nh * DN_HD), lambda p, i: (i, zoff + p)),
                  pl.BlockSpec((rb, 2 * DN_V_HEADS), lambda p, i: (i, 0)),
                  pl.BlockSpec((nh, nchunk, 1, DN_C), lambda p, i: (p, i, 0, 0)),
                  pl.BlockSpec((1, DN_HD), lambda p, i: (0, 0))],
        out_specs=pl.BlockSpec((rb, nh * DN_HD), lambda p, i: (i, p)),
        out_shape=jax.ShapeDtypeStruct((t, DN_VAL_W), BF16),
        scratch_shapes=[pltpu.VMEM((nh, DN_HD, DN_HD), F32)],
        compiler_params=_cparams(("arbitrary", "arbitrary")),
    )(qkv, qkv, qkv, proj, bg, gc_rows, norm_w.reshape(1, DN_HD))


def _router_kernel(h_ref, w_ref, b_ref, idx_ref, wt_ref):
    logits = _dot3(h_ref[...], w_ref[...]) + b_ref[...]
    tm, e = logits.shape
    lane = lax.broadcasted_iota(jnp.int32, (tm, e), 1).astype(F32)
    lane_o = lax.broadcasted_iota(jnp.int32, (tm, LANE), 1)
    idx_out = jnp.zeros((tm, LANE), jnp.int32)
    val_out = jnp.zeros((tm, LANE), F32)
    cur = logits
    vals = []
    for kth in range(TOP_K):
        m = jnp.max(cur, axis=-1, keepdims=True)
        sel = jnp.min(jnp.where(cur == m, lane, float(e)), axis=-1, keepdims=True)
        cur = jnp.where(lane == sel, -jnp.inf, cur)
        idx_out = jnp.where(lane_o == kth, sel.astype(jnp.int32), idx_out)
        vals.append(m)
    es = [jnp.exp(v - vals[0]) for v in vals]
    tot = es[0]
    for x in es[1:]:
        tot = tot + x
    for kth in range(TOP_K):
        val_out = jnp.where(lane_o == kth, es[kth] / tot, val_out)
    idx_ref[...] = idx_out
    wt_ref[...] = val_out


def _router(h, w_router, b_router, tm=512):
    t, d = h.shape
    tm = min(tm, t)
    e = w_router.shape[1]
    return pl.pallas_call(
        _router_kernel,
        grid=(t // tm,),
        in_specs=[pl.BlockSpec((tm, d), lambda i: (i, 0)),
                  pl.BlockSpec((d, e), lambda i: (0, 0)),
                  pl.BlockSpec((1, e), lambda i: (0, 0))],
        out_specs=[pl.BlockSpec((tm, LANE), lambda i: (i, 0)),
                   pl.BlockSpec((tm, LANE), lambda i: (i, 0))],
        out_shape=[jax.ShapeDtypeStruct((t, LANE), jnp.int32),
                   jax.ShapeDtypeStruct((t, LANE), F32)],
        compiler_params=_cparams(("arbitrary",)),
    )(h, w_router, b_router.reshape(1, e))


def _expert_kernel(be_ref, na_ref, tok_ref, h_ref, wg_ref, wu_ref, wd_ref, bg_ref, bu_ref, bd_ref,
                   o_ref, wgb_ref, wub_ref, wdb_ref, xbuf, sem):
    i = pl.program_id(0)
    na = na_ref[0]
    bm = xbuf.shape[1] // ROW_TILES

    def row_copy(blk, slot, r):
        src = pl.multiple_of(tok_ref[blk * bm + r] * ROW_TILES, ROW_TILES)
        dst = pl.multiple_of(r * ROW_TILES, ROW_TILES)
        return pltpu.make_async_copy(h_ref.at[pl.ds(src, ROW_TILES)],
                                     xbuf.at[slot, pl.ds(dst, ROW_TILES)], sem.at[slot])

    def start_block(blk, slot):
        def body(r, carry):
            row_copy(blk, slot, r).start()
            return carry
        lax.fori_loop(0, bm, body, 0, unroll=8)

    @pl.when(i == 0)
    def _():
        start_block(0, 0)

    @pl.when(i + 1 < na)
    def _():
        start_block(i + 1, (i + 1) % 2)

    prev = be_ref[jnp.maximum(i - 1, 0)]
    changed = jnp.logical_or(i == 0, be_ref[i] != prev)

    @pl.when(changed)
    def _():
        wgb_ref[...] = wg_ref[...].astype(BF16)
        wub_ref[...] = wu_ref[...].astype(BF16)
        wdb_ref[...] = wd_ref[...].astype(BF16)

    @pl.when(i < na)
    def _():
        slot = i % 2

        def wait_body(r, carry):
            row_copy(i, slot, r).wait()
            return carry
        lax.fori_loop(0, bm, wait_body, 0, unroll=8)
        x = _load_token_rows(xbuf.at[slot], bm).astype(BF16)
        gate = jnp.minimum(jnp.dot(x, wgb_ref[...], preferred_element_type=F32) + bg_ref[...],
                           SWIGLU_LIMIT)
        up = jnp.clip(jnp.dot(x, wub_ref[...], preferred_element_type=F32) + bu_ref[...],
                      -SWIGLU_LIMIT, SWIGLU_LIMIT)
        hid = gate * _sigmoid(SWIGLU_ALPHA * gate) * (up + 1.0)
        y = jnp.dot(hid.astype(BF16), wdb_ref[...], preferred_element_type=F32) + bd_ref[...]
        _store_token_rows(o_ref, y)

    @pl.when(i >= na)
    def _():
        o_ref[...] = jnp.zeros_like(o_ref)


def _experts(h_rows, slot_tok, block_e, n_active, layer, w_gate, b_gate, w_up, b_up, w_down, b_down):
    d = D_MODEL
    n_slots = slot_tok.shape[0]
    nb = n_slots // MOE_BM
    f = w_gate.shape[-1]
    l = layer

    def wspec(shape):
        return pl.BlockSpec((None, None) + shape, lambda i, be, na, tok: (l, be[i], 0, 0))

    grid_spec = pltpu.PrefetchScalarGridSpec(
        num_scalar_prefetch=3,
        grid=(nb,),
        in_specs=[pl.BlockSpec(memory_space=pl.ANY),
                  wspec((d, f)), wspec((d, f)), wspec((f, d)),
                  wspec((1, f)), wspec((1, f)), wspec((1, d))],
        out_specs=pl.BlockSpec((MOE_BM * ROW_TILES, LANE), lambda i, be, na, tok: (i, 0)),
        scratch_shapes=[pltpu.VMEM((d, f), BF16), pltpu.VMEM((d, f), BF16), pltpu.VMEM((f, d), BF16),
                        pltpu.VMEM((2, MOE_BM * ROW_TILES, LANE), F32), pltpu.SemaphoreType.DMA((2,))],
    )
    nl, ne = b_gate.shape[:2]
    return pl.pallas_call(
        _expert_kernel,
        grid_spec=grid_spec,
        out_shape=jax.ShapeDtypeStruct((n_slots * ROW_TILES, LANE), F32),
        compiler_params=_cparams(("arbitrary",)),
    )(block_e, n_active, slot_tok, h_rows, w_gate, w_up, w_down,
      b_gate.reshape(nl, ne, 1, f), b_up.reshape(nl, ne, 1, f), b_down.reshape(nl, ne, 1, d))


def _routing(top_idx):
    t = top_idx.shape[0]
    sel = jnp.sum(jax.nn.one_hot(top_idx, N_EXPERTS, dtype=jnp.int32), axis=1)
    counts = jnp.sum(sel, axis=0)
    before = jnp.cumsum(sel, axis=0) - sel
    padded = (counts + MOE_BM - 1) // MOE_BM * MOE_BM
    pend = jnp.cumsum(padded)
    pstart = pend - padded
    dest = pstart[top_idx] + jnp.take_along_axis(before, top_idx, axis=1)
    n_blocks = t * TOP_K // MOE_BM + N_EXPERTS
    n_active = (pend[-1] // MOE_BM).astype(jnp.int32)
    blk = jnp.arange(n_blocks, dtype=jnp.int32)
    blk = jnp.minimum(blk, jnp.maximum(n_active - 1, 0))
    block_e = jnp.sum((blk[:, None] * MOE_BM >= pend[None, :]).astype(jnp.int32), axis=1)
    block_e = jnp.minimum(block_e, N_EXPERTS - 1).astype(jnp.int32)
    tok = jnp.broadcast_to(jnp.arange(t, dtype=jnp.int32)[:, None], dest.shape)
    slot_tok = jnp.zeros((n_blocks * MOE_BM,), jnp.int32).at[dest.reshape(-1)].set(tok.reshape(-1))
    return dest.astype(jnp.int32), slot_tok, block_e, n_active.reshape(1)


def _combine_kernel(dest_ref, h_ref, w_ref, g_ref, b_ref, ys_ref, o_ref, obf_ref, buf, sem, *, tm):
    i = pl.program_id(0)
    n = pl.num_programs(0)

    def row_copy(blk, slot, r, k):
        src = pl.multiple_of(dest_ref[(blk * tm + r) * TOP_K + k] * ROW_TILES, ROW_TILES)
        dst = pl.multiple_of(r * ROW_TILES, ROW_TILES)
        return pltpu.make_async_copy(ys_ref.at[pl.ds(src, ROW_TILES)],
                                     buf.at[slot, k, pl.ds(dst, ROW_TILES)], sem.at[slot])

    def start_block(blk, slot):
        def body(r, carry):
            for k in range(TOP_K):
                row_copy(blk, slot, r, k).start()
            return carry
        lax.fori_loop(0, tm, body, 0, unroll=4)

    def wait_block(blk, slot):
        def body(r, carry):
            for k in range(TOP_K):
                row_copy(blk, slot, r, k).wait()
            return carry
        lax.fori_loop(0, tm, body, 0, unroll=4)

    @pl.when(i == 0)
    def _():
        start_block(0, 0)

    @pl.when(i + 1 < n)
    def _():
        start_block(i + 1, (i + 1) % 2)

    slot = i % 2
    wait_block(i, slot)
    w = w_ref[...]
    ffn = w[:, 0:1] * _load_token_rows(buf.at[slot, 0], tm)
    for k in range(1, TOP_K):
        ffn = ffn + w[:, k:k + 1] * _load_token_rows(buf.at[slot, k], tm)
    y = _layer_norm_rows(ALPHA * h_ref[...] + ffn, g_ref[...], b_ref[...])
    o_ref[...] = y
    obf_ref[...] = y.astype(BF16)


def _combine_norm(dest, h, top_w, ys, g, b, tm=128):
    t, d = h.shape
    tm = min(tm, t)
    row = pl.BlockSpec((tm, d), lambda i, dref: (i, 0))
    vec = pl.BlockSpec((1, d), lambda i, dref: (0, 0))
    grid_spec = pltpu.PrefetchScalarGridSpec(
        num_scalar_prefetch=1,
        grid=(t // tm,),
        in_specs=[row, pl.BlockSpec((tm, LANE), lambda i, dref: (i, 0)), vec, vec,
                  pl.BlockSpec(memory_space=pl.ANY)],
        out_specs=[row, row],
        scratch_shapes=[pltpu.VMEM((2, TOP_K, tm * ROW_TILES, LANE), F32),
                        pltpu.SemaphoreType.DMA((2,))],
    )
    return pl.pallas_call(
        functools.partial(_combine_kernel, tm=tm),
        grid_spec=grid_spec,
        out_shape=[jax.ShapeDtypeStruct((t, d), F32), jax.ShapeDtypeStruct((t, d), BF16)],
        compiler_params=_cparams(("arbitrary",)),
    )(dest.reshape(-1), h, top_w, g.reshape(1, d), b.reshape(1, d), ys)


def _layer(l, h, hb, p):
    t = h.shape[0]
    w_in_t = p['w_in_t']
    u = _matmul(hb, [(w_in_t, (l,), 0)], D_MODEL, F32, lambda acc: acc, tn=1024,
                w_rows_are_outputs=True)
    proj = _matmul(hb, [(w_in_t, (l,), OFF_QKV // 1024)], OFF_BETA - OFF_QKV, BF16, lambda acc: acc,
                   tn=1024, w_rows_are_outputs=True)
    gates = _matmul(hb, [(w_in_t[l, OFF_GATE_S5:], (), 0)], 2 * D_MODEL, BF16,
                    lambda acc: _sigmoid(acc), tn=1024, w_rows_are_outputs=True)
    bd = _matmul(hb, [(w_in_t[l, OFF_BETA:OFF_GATE_S5], (), 0)], 2 * DN_V_HEADS, F32,
                 lambda acc: acc, tn=2 * DN_V_HEADS, w_rows_are_outputs=True)

    tables = _s5_tables(p['s5_lam_re'][l], p['s5_lam_im'][l], p['s5_log_dt'][l], p['s5_b_re'][l],
                        p['s5_b_im'][l], p['s5_c_re'][l], p['s5_c_im'][l], p['s5_d'][l], t // S5_L)
    y = _s5_apply(u, tables)
    part = _matmul(y, [(p['w_glu_a'], (l,), 0), (p['w_glu_b'], (l,), 0)], D_MODEL, BF16,
                   lambda a, b, g: a * _sigmoid(b) * g.astype(F32), extras=[(gates, 0)])

    qkv = _dn_conv(proj, p['dn_conv_w'][l].T)
    bg = _dn_gates(bd, p['dn_a_log'][l], p['dn_dt_bias'][l])
    gc_rows = bg[:, DN_V_HEADS:].T.reshape(DN_V_HEADS, t // DN_C, 1, DN_C)
    o = _delta_rule(qkv, proj, bg, gc_rows, p['dn_norm_w'][l])
    merged = _matmul(o, [(p['w_dn_out'], (l,), 0)], D_MODEL, BF16,
                     lambda acc, g, s: acc * g.astype(F32) + s.astype(F32),
                     extras=[(gates, D_MODEL // 512), (part, 0)])
    mix = _matmul(merged, [(p['w_mix_out'], (l,), 0)], D_MODEL, F32, lambda acc: acc, tn=1024)
    h, hb, h_rows = _deepnorm(h, mix, p['ln1_g'][l], p['ln1_b'][l])

    top_idx, top_w = _router(h, p['w_router'][l], p['b_router'][l])
    dest, slot_tok, block_e, n_active = _routing(top_idx[:, :TOP_K])
    ys = _experts(h_rows, slot_tok, block_e, n_active, l, p['w_gate'], p['b_gate'], p['w_up'],
                  p['b_up'], p['w_down'], p['b_down'])
    return _combine_norm(dest, h, top_w, ys, p['ln2_g'][l], p['ln2_b'][l])


def kernel(x, w_in, dn_conv_w, dn_a_log, dn_dt_bias, dn_norm_w, w_dn_out, s5_lam_re, s5_lam_im, s5_log_dt, s5_b_re, s5_b_im, s5_c_re, s5_c_im, s5_d, w_glu_a, w_glu_b, w_mix_out, ln1_g, ln1_b, w_router, b_router, w_gate, b_gate, w_up, b_up, w_down, b_down, ln2_g, ln2_b):
    p = dict(w_in_t=jnp.swapaxes(w_in, 1, 2), dn_conv_w=dn_conv_w, dn_a_log=dn_a_log, dn_dt_bias=dn_dt_bias,
             dn_norm_w=dn_norm_w, w_dn_out=w_dn_out, s5_lam_re=s5_lam_re, s5_lam_im=s5_lam_im,
             s5_log_dt=s5_log_dt, s5_b_re=s5_b_re, s5_b_im=s5_b_im, s5_c_re=s5_c_re,
             s5_c_im=s5_c_im, s5_d=s5_d, w_glu_a=w_glu_a, w_glu_b=w_glu_b, w_mix_out=w_mix_out,
             ln1_g=ln1_g, ln1_b=ln1_b, w_router=w_router, b_router=b_router, w_gate=w_gate,
             b_gate=b_gate, w_up=w_up, b_up=b_up, w_down=w_down, b_down=b_down,
             ln2_g=ln2_g, ln2_b=ln2_b)
    bsz, t, d = x.shape
    h = x.reshape(bsz * t, d)
    hb = h.astype(BF16)
    for l in range(w_in.shape[0]):
        h, hb = _layer(l, h, hb, p)
    return h.reshape(bsz, t, d)
```

```python
import functools
import math

import jax
import jax.numpy as jnp
from jax import lax
from jax.experimental import pallas as pl
from jax.experimental.pallas import tpu as pltpu

F32 = jnp.float32
BF16 = jnp.bfloat16

D_MODEL = 2048
DEPTH = 4
S5_P = 16
S5_N = 64
S5_G = D_MODEL // S5_P
S5_L = 16
DN_QK_HEADS = 16
DN_V_HEADS = 32
DN_HD = 128
DN_KEY_W = DN_QK_HEADS * DN_HD
DN_VAL_W = DN_V_HEADS * DN_HD
DN_CONV_CH = 2 * DN_KEY_W + DN_VAL_W
DN_CONV = 4
DN_C = 128
N_EXPERTS = 32
TOP_K = 4
D_EXPERT = 512
SWIGLU_LIMIT = 7.0
SWIGLU_ALPHA = 1.702
MOE_BM = 256
ALPHA = (2 * DEPTH) ** 0.25
LN_EPS = 1e-5
RMS_EPS = 1e-6
L2_EPS = 1e-6
S5_MAX_REAL = -1e-4
OFF_QKV = D_MODEL
OFF_Z = OFF_QKV + DN_CONV_CH
OFF_BETA = OFF_Z + DN_VAL_W
OFF_GATE_S5 = OFF_BETA + 2 * DN_V_HEADS
IN_COLS = OFF_GATE_S5 + 2 * D_MODEL
LANE = 128
VMEM_LIMIT = 56 * 1024 * 1024


def _cparams(sem):
    return pltpu.CompilerParams(dimension_semantics=sem, vmem_limit_bytes=VMEM_LIMIT)


_NT = (((1,), (1,)), ((), ()))
_NN = (((1,), (0,)), ((), ()))


def _mm_kernel(*refs, nw, nx, epilogue, dims):
    a_ref = refs[0]
    w_refs = refs[1:1 + nw]
    x_refs = refs[1 + nw:1 + nw + nx]
    o_ref = refs[1 + nw + nx]
    wbf_refs = refs[2 + nw + nx:]

    @pl.when(pl.program_id(1) == 0)
    def _():
        for w_ref, wbf_ref in zip(w_refs, wbf_refs):
            wbf_ref[...] = w_ref[...].astype(BF16)

    a = a_ref[...].astype(BF16)
    accs = [lax.dot_general(a, wbf[...], dims, preferred_element_type=F32) for wbf in wbf_refs]
    o_ref[...] = epilogue(*accs, *[x[...] for x in x_refs]).astype(o_ref.dtype)


def _matmul(a, ws, n_out, out_dtype, epilogue, extras=(), tm=512, tn=512, w_rows_are_outputs=False):
    m, k = a.shape
    tm = min(tm, m)
    grid = (n_out // tn, m // tm)
    in_specs = [pl.BlockSpec((tm, k), lambda j, i: (i, 0))]
    args = [a]
    wshape = (tn, k) if w_rows_are_outputs else (k, tn)
    for w, lead, off in ws:
        nlead = len(lead)
        if w_rows_are_outputs:
            imap = lambda j, i, lead=lead, off=off: tuple(lead) + (j + off, 0)
        else:
            imap = lambda j, i, lead=lead, off=off: tuple(lead) + (0, j + off)
        in_specs.append(pl.BlockSpec((None,) * nlead + wshape, imap))
        args.append(w)
    for x, off in extras:
        in_specs.append(pl.BlockSpec((tm, tn), lambda j, i, off=off: (i, j + off)))
        args.append(x)
    return pl.pallas_call(
        functools.partial(_mm_kernel, nw=len(ws), nx=len(extras), epilogue=epilogue,
                          dims=_NT if w_rows_are_outputs else _NN),
        grid=grid,
        in_specs=in_specs,
        out_specs=pl.BlockSpec((tm, tn), lambda j, i: (i, j)),
        out_shape=jax.ShapeDtypeStruct((m, n_out), out_dtype),
        scratch_shapes=[pltpu.VMEM(wshape, BF16) for _ in ws],
        compiler_params=_cparams(("arbitrary", "arbitrary")),
    )(*args)


def _sigmoid(x):
    return 1.0 / (1.0 + jnp.exp(-x))


def _split_bf16(x):
    hi = x.astype(BF16)
    lo = (x - hi.astype(F32)).astype(BF16)
    return hi, lo


def _dot3(a, b, dims=_NN):
    ah, al = _split_bf16(a)
    bh, bl = _split_bf16(b)
    return (lax.dot_general(ah, bh, dims, preferred_element_type=F32)
            + lax.dot_general(ah, bl, dims, preferred_element_type=F32)
            + lax.dot_general(al, bh, dims, preferred_element_type=F32))


ROW_TILES = D_MODEL // LANE


def _store_token_rows(ref, y):
    n = y.shape[0]
    for j in range(ROW_TILES):
        ref[pl.ds(j, n, stride=ROW_TILES), :] = y[:, j * LANE:(j + 1) * LANE]


def _load_token_rows(ref, n):
    return jnp.concatenate([ref[pl.ds(j, n, stride=ROW_TILES), :] for j in range(ROW_TILES)], axis=1)


def _layer_norm_rows(x, g, b):
    mu = jnp.mean(x, axis=-1, keepdims=True)
    xc = x - mu
    var = jnp.mean(xc * xc, axis=-1, keepdims=True)
    return xc * lax.rsqrt(var + LN_EPS) * g + b


def _ln_kernel(h_ref, r_ref, g_ref, b_ref, o_ref, obf_ref, orow_ref):
    y = _layer_norm_rows(ALPHA * h_ref[...] + r_ref[...], g_ref[...], b_ref[...])
    o_ref[...] = y
    obf_ref[...] = y.astype(BF16)
    _store_token_rows(orow_ref, y)


def _deepnorm(h, r, g, b, tm=256):
    t, d = h.shape
    tm = min(tm, t)
    row = pl.BlockSpec((tm, d), lambda i: (i, 0))
    vec = pl.BlockSpec((1, d), lambda i: (0, 0))
    return pl.pallas_call(
        _ln_kernel,
        grid=(t // tm,),
        in_specs=[row, row, vec, vec],
        out_specs=[row, row, pl.BlockSpec((tm * ROW_TILES, LANE), lambda i: (i, 0))],
        out_shape=[jax.ShapeDtypeStruct((t, d), F32), jax.ShapeDtypeStruct((t, d), BF16),
                   jax.ShapeDtypeStruct((t * ROW_TILES, LANE), F32)],
        compiler_params=_cparams(("arbitrary",)),
    )(h, r, g.reshape(1, d), b.reshape(1, d))


def _s5_tables(lam_re, lam_im, log_dt, b_re, b_im, c_re, c_im, d_skip, n_chunks):
    L = S5_L
    g = lam_re.shape[0]
    lre = jnp.minimum(lam_re, S5_MAX_REAL)
    lim = lam_im
    dt = jnp.exp(log_dt)[:, None]
    ks = jnp.arange(L + 1, dtype=F32)[:, None, None]
    mag = jnp.exp(lre * dt * ks)
    pr = mag * jnp.cos(lim * dt * ks)
    pi = mag * jnp.sin(lim * dt * ks)
    a_re, a_im = pr[1], pi[1]
    den = lre * lre + lim * lim
    f_re = ((a_re - 1.0) * lre + a_im * lim) / den
    f_im = (a_im * lre - (a_re - 1.0) * lim) / den
    bb_re = f_re[..., None] * b_re - f_im[..., None] * b_im
    bb_im = f_re[..., None] * b_im + f_im[..., None] * b_re
    bb = jnp.concatenate([jnp.transpose(bb_re, (0, 2, 1)), jnp.transpose(bb_im, (0, 2, 1))], axis=-1)
    cc = jnp.concatenate([c_re, c_im], axis=-1)
    qr = jnp.transpose(pr[L - 1::-1], (1, 0, 2))
    qi = jnp.transpose(pi[L - 1::-1], (1, 0, 2))
    p1 = jnp.concatenate([qr, qr], axis=-1)
    p2 = jnp.concatenate([-qi, qi], axis=-1)
    ur = jnp.transpose(pr[1:], (1, 0, 2))
    ui = jnp.transpose(pi[1:], (1, 0, 2))
    q1 = jnp.concatenate([ur, -ur], axis=-1)
    q2 = jnp.concatenate([-ui, -ui], axis=-1)
    nlev = max(1, int(math.log2(n_chunks)))
    mr, mi = pr[L], pi[L]
    m1, m2 = [], []
    for _ in range(nlev):
        m1.append(jnp.concatenate([mr, mr], axis=-1))
        m2.append(jnp.concatenate([-mi, mi], axis=-1))
        mr, mi = mr * mr - mi * mi, 2.0 * mr * mi
    pad = [jnp.zeros_like(m1[0])] * (16 - nlev)
    pw1 = jnp.stack(m1 + pad, axis=1)
    pw2 = jnp.stack(m2 + pad, axis=1)
    dsk = jnp.tile(d_skip, (1, L)).reshape(g, 1, L * S5_P)
    return p1, p2, q1, q2, bb, cc, pw1, pw2, dsk


def _gelu_tanh(y):
    return 0.5 * y * (1.0 + jnp.tanh(0.7978845608028654 * (y + 0.044715 * y * y * y)))


def _rep_rows(x, n):
    r, w = x.shape
    return jnp.broadcast_to(x[:, None, :], (r, n, w)).reshape(r * n, w)


def _tile_rows(x, n):
    r, w = x.shape
    return jnp.broadcast_to(x[None, :, :], (n, r, w)).reshape(n * r, w)


def _s5_kernel(u_ref, p1_ref, p2_ref, q1_ref, q2_ref, bb_ref, cc_ref, pw1_ref, pw2_ref, d_ref,
               o_ref, perm_ref, *, nlev, gb):
    L, P, N = S5_L, S5_P, S5_N
    lp = L * P
    half = (L // 2) * LANE
    nc = u_ref.shape[0] // L

    @pl.when(pl.program_id(0) == 0)
    def _():
        r = lax.broadcasted_iota(jnp.int32, (half, half), 0)
        c = lax.broadcasted_iota(jnp.int32, (half, half), 1)
        dst = ((r >> 4) & 7) * LANE + (r >> 7) * P + (r & 15)
        perm_ref[...] = jnp.where(c == dst, 1.0, 0.0).astype(BF16)

    perm = perm_ref[...]
    v = []
    for th in range(2):
        xs = [u_ref[pl.ds(th * 8 + tl, nc, stride=L), :].astype(BF16) for tl in range(8)]
        v.append(jnp.dot(jnp.concatenate(xs, axis=1), perm, preferred_element_type=F32).astype(BF16))

    row = lax.broadcasted_iota(jnp.int32, (nc, 2 * N), 0)
    lane_n = lax.broadcasted_iota(jnp.int32, (1, 2 * N), 1)
    sign = jnp.where(lane_n < N, 1.0, -1.0)
    rblk = lax.broadcasted_iota(jnp.int32, (lp, lp), 0) >> 4
    cblk = lax.broadcasted_iota(jnp.int32, (lp, lp), 1) >> 4
    z = [[], []]
    for gi in range(gb):
        ug = jnp.concatenate([v[0][:, gi * LANE:(gi + 1) * LANE],
                              v[1][:, gi * LANE:(gi + 1) * LANE]], axis=1)
        bbg = bb_ref[gi]
        ccg = cc_ref[gi]
        bmat = (_rep_rows(p1_ref[gi], P) * _tile_rows(bbg, L)
                + _rep_rows(p2_ref[gi], P) * _tile_rows(pltpu.roll(bbg, N, axis=1), L))
        cmt = (_rep_rows(q1_ref[gi], P) * _tile_rows(ccg, L)
               + _rep_rows(q2_ref[gi], P) * _tile_rows(pltpu.roll(ccg, N, axis=1), L))
        w = _dot3(bmat, _tile_rows(ccg * sign, L), _NT)
        tmat = jnp.zeros((lp, lp), F32)
        for t in range(L):
            shift = (lp - (L - 1 - t) * P) % lp
            tmat = jnp.where((cblk == t) & (rblk <= t), pltpu.roll(w, shift, axis=0), tmat)
        x = jnp.dot(ug, bmat.astype(BF16), preferred_element_type=F32)
        for lev in range(nlev):
            d = 1 << lev
            m1 = pw1_ref[gi, lev:lev + 1, :]
            m2 = pw2_ref[gi, lev:lev + 1, :]
            sh = jnp.where(row >= d, pltpu.roll(x, d, axis=0), 0.0)
            x = x + m1 * sh + m2 * pltpu.roll(sh, N, axis=1)
        sprev = jnp.where(row >= 1, pltpu.roll(x, 1, axis=0), 0.0)
        y = (jnp.dot(ug, tmat.astype(BF16), preferred_element_type=F32)
             + lax.dot_general(sprev.astype(BF16), cmt.astype(BF16), _NT, preferred_element_type=F32)
             + d_ref[gi] * ug.astype(F32))
        yg = _gelu_tanh(y).astype(BF16)
        z[0].append(yg[:, :LANE])
        z[1].append(yg[:, LANE:])
    for th in range(2):
        yp = lax.dot_general(jnp.concatenate(z[th], axis=1), perm, _NT, preferred_element_type=F32)
        for tl in range(8):
            o_ref[pl.ds(th * 8 + tl, nc, stride=L), :] = yp[:, tl * LANE:(tl + 1) * LANE]


def _s5_apply(u, tables, layer):
    t, d = u.shape
    gb = LANE // S5_P
    nc = t // S5_L
    nlev = int(math.log2(nc))
    assert (1 << nlev) == nc and nlev <= 16
    first = layer * (d // LANE)
    tab = pl.BlockSpec((gb, 16, 2 * S5_N), lambda i: (first + i, 0, 0))
    return pl.pallas_call(
        functools.partial(_s5_kernel, nlev=nlev, gb=gb),
        grid=(d // LANE,),
        in_specs=[pl.BlockSpec((t, LANE), lambda i: (0, i))] + [tab] * 8
                 + [pl.BlockSpec((gb, 1, S5_L * S5_P), lambda i: (first + i, 0, 0))],
        out_specs=pl.BlockSpec((t, LANE), lambda i: (0, i)),
        out_shape=jax.ShapeDtypeStruct((t, d), F32),
        scratch_shapes=[pltpu.VMEM((8 * LANE, 8 * LANE), BF16)],
        compiler_params=_cparams(("arbitrary",)),
    )(u, *tables)


def _conv_kernel(x_ref, p_ref, w_ref, o_ref, xs_ref, *, nq, nqk):
    j = pl.program_id(0)
    i = pl.program_id(1)
    tm, tc = x_ref.shape
    prev = p_ref[...].astype(F32)
    xs_ref[0:8, :] = jnp.where(i > 0, prev[8:16], 0.0)
    xs_ref[8:8 + tm, :] = x_ref[...].astype(F32)
    w = w_ref[...]
    acc = xs_ref[8:8 + tm, :] * w[DN_CONV - 1:DN_CONV, :]
    for s in range(1, DN_CONV):
        acc = acc + xs_ref[8 - s:8 - s + tm, :] * w[DN_CONV - 1 - s:DN_CONV - s, :]
    y = acc * _sigmoid(acc)
    @pl.when(j < nqk)
    def _():
        qscale = jnp.where(j < nq, DN_HD ** -0.5, 1.0)
        for hh in range(tc // DN_HD):
            blk = y[:, hh * DN_HD:(hh + 1) * DN_HD]
            ss = jnp.sum(blk * blk, axis=-1, keepdims=True)
            o_ref[:, hh * DN_HD:(hh + 1) * DN_HD] = (blk * (lax.rsqrt(ss + L2_EPS) * qscale)).astype(o_ref.dtype)

    @pl.when(j >= nqk)
    def _():
        o_ref[...] = y.astype(o_ref.dtype)


def _dn_conv(proj, conv_wt, tm=512, tc=512):
    t = proj.shape[0]
    tm = min(tm, t)
    pb = tm // 16
    return pl.pallas_call(
        functools.partial(_conv_kernel, nq=DN_KEY_W // tc, nqk=2 * DN_KEY_W // tc),
        grid=(DN_CONV_CH // tc, t // tm),
        in_specs=[pl.BlockSpec((tm, tc), lambda j, i: (i, j)),
                  pl.BlockSpec((16, tc), lambda j, i: (jnp.maximum(i * pb - 1, 0), j)),
                  pl.BlockSpec((DN_CONV, tc), lambda j, i: (0, j))],
        out_specs=pl.BlockSpec((tm, tc), lambda j, i: (i, j)),
        out_shape=jax.ShapeDtypeStruct((t, DN_CONV_CH), BF16),
        scratch_shapes=[pltpu.VMEM((tm + 8, tc), F32)],
        compiler_params=_cparams(("arbitrary", "arbitrary")),
    )(proj, proj, conv_wt)


def _dn_gate_kernel(x_ref, a_ref, dtb_ref, o_ref):
    x = x_ref[...]
    tm, w = x.shape
    lane = lax.broadcasted_iota(jnp.int32, (tm, w), 1)
    row = lax.broadcasted_iota(jnp.int32, (tm, w), 0)
    xs = x + dtb_ref[...]
    softplus = jnp.maximum(xs, 0.0) + jnp.log(1.0 + jnp.exp(-jnp.abs(xs)))
    g = -jnp.exp(a_ref[...]) * softplus
    pos = row & (DN_C - 1)
    d = 1
    while d < DN_C:
        g = g + jnp.where(pos >= d, pltpu.roll(g, d, axis=0), 0.0)
        d *= 2
    o_ref[...] = jnp.where(lane < DN_V_HEADS, _sigmoid(x), g)


def _dn_gates(bd, a_log, dt_bias, tm=512):
    t = bd.shape[0]
    tm = min(tm, t)
    zeros = jnp.zeros((DN_V_HEADS,), F32)
    a2 = jnp.concatenate([zeros, a_log]).reshape(1, -1)
    b2 = jnp.concatenate([zeros, dt_bias]).reshape(1, -1)
    w = 2 * DN_V_HEADS
    return pl.pallas_call(
        _dn_gate_kernel,
        grid=(t // tm,),
        in_specs=[pl.BlockSpec((tm, w), lambda i: (i, 0)),
                  pl.BlockSpec((1, w), lambda i: (0, 0)),
                  pl.BlockSpec((1, w), lambda i: (0, 0))],
        out_specs=pl.BlockSpec((tm, w), lambda i: (i, 0)),
        out_shape=jax.ShapeDtypeStruct((t, w), F32),
        compiler_params=_cparams(("arbitrary",)),
    )(bd, a2, b2)


def _bdot(a, b):
    return jnp.einsum('bij,bjk->bik', a.astype(BF16), b.astype(BF16), preferred_element_type=F32)


def _unit_lower_inverse(a, ii, jj):
    c = a.shape[-1]
    eye = (ii == jj).astype(F32)
    p = jnp.where((ii >> 3) == (jj >> 3), -a, 0.0)
    p2 = _bdot(p, p)
    t = eye + p
    t = t + _bdot(t, p2)
    p4 = _bdot(p2, p2)
    t = t + _bdot(t, p4)
    s = 8
    sh = 3
    while s < c:
        bi = ii >> sh
        bj = jj >> sh
        off = jnp.where(((bi & 1) == 1) & (bj == bi - 1), a, 0.0)
        t = t - _bdot(_bdot(t, off), t)
        s *= 2
        sh += 1
    return t


def _delta_kernel(q_ref, k_ref, v_ref, z_ref, bg_ref, gr_ref, nw_ref, o_ref, s_ref, *, nchunk, nh):
    pp = pl.program_id(0)
    i = pl.program_id(1)

    @pl.when(i == 0)
    def _():
        s_ref[...] = jnp.zeros_like(s_ref)

    c = DN_C
    hd = DN_HD
    bg = bg_ref[...]
    lane = lax.broadcasted_iota(jnp.int32, bg.shape, 1)
    ii = lax.broadcasted_iota(jnp.int32, (c, c), 0)
    jj = lax.broadcasted_iota(jnp.int32, (c, c), 1)
    nw = nw_ref[...]

    q3 = q_ref[...].reshape(nchunk, c, hd)
    k3 = k_ref[...].reshape(nchunk, c, hd)
    kk = jnp.einsum('cid,cjd->cij', k3, k3, preferred_element_type=F32)
    qk = jnp.einsum('cid,cjd->cij', q3, k3, preferred_element_type=F32)
    kf = k3.astype(F32)
    qf = q3.astype(F32)
    a_l, attn_l, rhs_l, qd_l, kd_l, gl_l = [], [], [], [], [], []
    for j in range(nh):
        hh = nh * pp + j
        beta = jnp.sum(jnp.where(lane == hh, bg, 0.0), axis=1, keepdims=True).reshape(nchunk, c, 1)
        gc = jnp.sum(jnp.where(lane == hh + DN_V_HEADS, bg, 0.0), axis=1,
                     keepdims=True).reshape(nchunk, c, 1)
        gr = gr_ref[j]
        decay = jnp.exp(jnp.where(ii >= jj, gc - gr, -jnp.inf))
        a_l.append(jnp.where(ii > jj, kk * decay, 0.0) * beta)
        attn_l.append((qk * decay).astype(BF16))
        eg = jnp.exp(gc)
        v = v_ref[:, j * hd:(j + 1) * hd].astype(F32).reshape(nchunk, c, hd)
        rhs_l.append(jnp.concatenate([v * beta, kf * (beta * eg)], axis=-1))
        qd_l.append((qf * eg).astype(BF16))
        g_last = gr[:, :, c - 1:c]
        kd_l.append(kf * jnp.exp(g_last - gc))
        gl_l.append(jnp.exp(g_last))
    tinv = _unit_lower_inverse(jnp.concatenate(a_l, axis=0), ii, jj)
    sol = _bdot(tinv, jnp.concatenate(rhs_l, axis=0))

    s = [s_ref[j] for j in range(nh)]
    for ci in range(nchunk):
        r0 = ci * c
        for j in range(nh):
            b = j * nchunk + ci
            u_c = sol[b, :, :hd]
            w_c = sol[b, :, hd:]
            sb = s[j].astype(BF16)
            v_new = u_c - jnp.dot(w_c.astype(BF16), sb, preferred_element_type=F32)
            vb = v_new.astype(BF16)
            o = (jnp.dot(qd_l[j][ci], sb, preferred_element_type=F32)
                 + jnp.dot(attn_l[j][ci], vb, preferred_element_type=F32))
            s[j] = s[j] * gl_l[j][ci] + jnp.dot(kd_l[j][ci].T.astype(BF16), vb,
                                               preferred_element_type=F32)
            ms = jnp.mean(o * o, axis=-1, keepdims=True)
            z = z_ref[r0:r0 + c, j * hd:(j + 1) * hd].astype(F32)
            o = o * lax.rsqrt(ms + RMS_EPS) * nw * (z * _sigmoid(z))
            o_ref[r0:r0 + c, j * hd:(j + 1) * hd] = o.astype(o_ref.dtype)
    for j in range(nh):
        s_ref[j] = s[j]


def _delta_rule(qkv, proj, bg, gc_rows, norm_w, rb=1024):
    t = qkv.shape[0]
    rb = min(rb, t)
    nchunk = rb // DN_C
    nh = DN_V_HEADS // DN_QK_HEADS
    kq = DN_KEY_W // DN_HD
    voff = 2 * DN_KEY_W // (nh * DN_HD)
    zoff = DN_CONV_CH // (nh * DN_HD)
    return pl.pallas_call(
        functools.partial(_delta_kernel, nchunk=nchunk, nh=nh),
        grid=(DN_QK_HEADS, t // rb),
        in_specs=[pl.BlockSpec((rb, DN_HD), lambda p, i: (i, p)),
                  pl.BlockSpec((rb, DN_HD), lambda p, i: (i, kq + p)),
                  pl.BlockSpec((rb, nh * DN_HD), lambda p, i: (i, voff + p)),
                  pl.BlockSpec((rb, nh * DN_HD), lambda p, i: (i, zoff + p)),
                  pl.BlockSpec((rb, 2 * DN_V_HEADS), lambda p, i: (i, 0)),
                  pl.BlockSpec((nh, nchunk, 1, DN_C), lambda p, i: (p, i, 0, 0)),
                  pl.BlockSpec((1, DN_HD), lambda p, i: (0, 0))],
        out_specs=pl.BlockSpec((rb, nh * DN_HD), lambda p, i: (i, p)),
        out_shape=jax.ShapeDtypeStruct((t, DN_VAL_W), BF16),
        scratch_shapes=[pltpu.VMEM((nh, DN_HD, DN_HD), F32)],
        compiler_params=_cparams(("arbitrary", "arbitrary")),
    )(qkv, qkv, qkv, proj, bg, gc_rows, norm_w.reshape(1, DN_HD))


def _router_kernel(h_ref, w_ref, b_ref, idx_ref, wt_ref):
    logits = _dot3(h_ref[...], w_ref[...]) + b_ref[...]
    tm, e = logits.shape
    lane = lax.broadcasted_iota(jnp.int32, (tm, e), 1).astype(F32)
    lane_o = lax.broadcasted_iota(jnp.int32, (tm, LANE), 1)
    idx_out = jnp.zeros((tm, LANE), jnp.int32)
    val_out = jnp.zeros((tm, LANE), F32)
    cur = logits
    vals = []
    for kth in range(TOP_K):
        m = jnp.max(cur, axis=-1, keepdims=True)
        sel = jnp.min(jnp.where(cur == m, lane, float(e)), axis=-1, keepdims=True)
        cur = jnp.where(lane == sel, -jnp.inf, cur)
        idx_out = jnp.where(lane_o == kth, sel.astype(jnp.int32), idx_out)
        vals.append(m)
    es = [jnp.exp(v - vals[0]) for v in vals]
    tot = es[0]
    for x in es[1:]:
        tot = tot + x
    for kth in range(TOP_K):
        val_out = jnp.where(lane_o == kth, es[kth] / tot, val_out)
    idx_ref[...] = idx_out
    wt_ref[...] = val_out


def _router(h, w_router, b_router, tm=512):
    t, d = h.shape
    tm = min(tm, t)
    e = w_router.shape[1]
    return pl.pallas_call(
        _router_kernel,
        grid=(t // tm,),
        in_specs=[pl.BlockSpec((tm, d), lambda i: (i, 0)),
                  pl.BlockSpec((d, e), lambda i: (0, 0)),
                  pl.BlockSpec((1, e), lambda i: (0, 0))],
        out_specs=[pl.BlockSpec((tm, LANE), lambda i: (i, 0)),
                   pl.BlockSpec((tm, LANE), lambda i: (i, 0))],
        out_shape=[jax.ShapeDtypeStruct((t, LANE), jnp.int32),
                   jax.ShapeDtypeStruct((t, LANE), F32)],
        compiler_params=_cparams(("arbitrary",)),
    )(h, w_router, b_router.reshape(1, e))


def _expert_kernel(be_ref, na_ref, tok_ref, brun_ref, bnext_ref, h_ref, wg_ref, wu_ref, wd_ref,
                   bg_ref, bu_ref, bd_ref, o_ref, wgb_ref, wub_ref, wdb_ref, xbuf, sem,
                   wgf_ref, wuf_ref, wdf_ref, wsem, *, layer):
    i = pl.program_id(0)
    na = na_ref[0]
    bm = xbuf.shape[1] // ROW_TILES

    def weight_copies(e, s):
        return [pltpu.make_async_copy(src.at[layer, e], dst.at[s], wsem.at[s])
                for src, dst in ((wg_ref, wgf_ref), (wu_ref, wuf_ref), (wd_ref, wdf_ref))]

    @pl.when(i == 0)
    def _():
        for c in weight_copies(be_ref[0], 0):
            c.start()

    def row_copy(blk, slot, r):
        src = pl.multiple_of(tok_ref[blk * bm + r] * ROW_TILES, ROW_TILES)
        dst = pl.multiple_of(r * ROW_TILES, ROW_TILES)
        return pltpu.make_async_copy(h_ref.at[pl.ds(src, ROW_TILES)],
                                     xbuf.at[slot, pl.ds(dst, ROW_TILES)], sem.at[slot])

    def start_block(blk, slot):
        def body(r, carry):
            row_copy(blk, slot, r).start()
            return carry
        lax.fori_loop(0, bm, body, 0, unroll=8)

    @pl.when(i == 0)
    def _():
        start_block(0, 0)

    @pl.when(i + 1 < na)
    def _():
        start_block(i + 1, (i + 1) % 2)

    prev = be_ref[jnp.maximum(i - 1, 0)]
    changed = jnp.logical_or(i == 0, be_ref[i] != prev)

    @pl.when(changed)
    def _():
        ws = brun_ref[i] % 2
        for c in weight_copies(be_ref[i], ws):
            c.wait()
        wgb_ref[...] = wgf_ref[ws].astype(BF16)
        wub_ref[...] = wuf_ref[ws].astype(BF16)
        wdb_ref[...] = wdf_ref[ws].astype(BF16)
        nxt = bnext_ref[i]

        @pl.when(nxt >= 0)
        def _():
            for c in weight_copies(nxt, 1 - ws):
                c.start()

    @pl.when(i < na)
    def _():
        slot = i % 2

        def wait_body(r, carry):
            row_copy(i, slot, r).wait()
            return carry
        lax.fori_loop(0, bm, wait_body, 0, unroll=8)
        x = _load_token_rows(xbuf.at[slot], bm).astype(BF16)
        gate = jnp.minimum(jnp.dot(x, wgb_ref[...], preferred_element_type=F32) + bg_ref[...],
                           SWIGLU_LIMIT)
        up = jnp.clip(jnp.dot(x, wub_ref[...], preferred_element_type=F32) + bu_ref[...],
                      -SWIGLU_LIMIT, SWIGLU_LIMIT)
        hid = gate * _sigmoid(SWIGLU_ALPHA * gate) * (up + 1.0)
        y = jnp.dot(hid.astype(BF16), wdb_ref[...], preferred_element_type=F32) + bd_ref[...]
        _store_token_rows(o_ref, y)

    @pl.when(i >= na)
    def _():
        o_ref[...] = jnp.zeros_like(o_ref)


def _experts(h_rows, slot_tok, block_e, n_active, block_run, block_next, layer,
             w_gate, b_gate, w_up, b_up, w_down, b_down):
    d = D_MODEL
    n_slots = slot_tok.shape[0]
    nb = n_slots // MOE_BM
    f = w_gate.shape[-1]
    l = layer

    def bspec(shape):
        return pl.BlockSpec((None, None) + shape, lambda i, be, *_: (l, be[i], 0, 0))

    hbm = pl.BlockSpec(memory_space=pl.ANY)
    grid_spec = pltpu.PrefetchScalarGridSpec(
        num_scalar_prefetch=5,
        grid=(nb,),
        in_specs=[hbm, hbm, hbm, hbm, bspec((1, f)), bspec((1, f)), bspec((1, d))],
        out_specs=pl.BlockSpec((MOE_BM * ROW_TILES, LANE), lambda i, *_: (i, 0)),
        scratch_shapes=[pltpu.VMEM((d, f), BF16), pltpu.VMEM((d, f), BF16), pltpu.VMEM((f, d), BF16),
                        pltpu.VMEM((2, MOE_BM * ROW_TILES, LANE), F32), pltpu.SemaphoreType.DMA((2,)),
                        pltpu.VMEM((2, d, f), F32), pltpu.VMEM((2, d, f), F32), pltpu.VMEM((2, f, d), F32),
                        pltpu.SemaphoreType.DMA((2,))],
    )
    nl, ne = b_gate.shape[:2]
    return pl.pallas_call(
        functools.partial(_expert_kernel, layer=l),
        grid_spec=grid_spec,
        out_shape=jax.ShapeDtypeStruct((n_slots * ROW_TILES, LANE), F32),
        compiler_params=_cparams(("arbitrary",)),
    )(block_e, n_active, slot_tok, block_run, block_next, h_rows, w_gate, w_up, w_down,
      b_gate.reshape(nl, ne, 1, f), b_up.reshape(nl, ne, 1, f), b_down.reshape(nl, ne, 1, d))


def _routing(top_idx):
    t = top_idx.shape[0]
    sel = jnp.sum(jax.nn.one_hot(top_idx, N_EXPERTS, dtype=jnp.int32), axis=1)
    counts = jnp.sum(sel, axis=0)
    before = jnp.cumsum(sel, axis=0) - sel
    padded = (counts + MOE_BM - 1) // MOE_BM * MOE_BM
    pend = jnp.cumsum(padded)
    pstart = pend - padded
    dest = pstart[top_idx] + jnp.take_along_axis(before, top_idx, axis=1)
    n_blocks = t * TOP_K // MOE_BM + N_EXPERTS
    n_active = (pend[-1] // MOE_BM).astype(jnp.int32)
    blk = jnp.arange(n_blocks, dtype=jnp.int32)
    blk = jnp.minimum(blk, jnp.maximum(n_active - 1, 0))
    block_e = jnp.sum((blk[:, None] * MOE_BM >= pend[None, :]).astype(jnp.int32), axis=1)
    block_e = jnp.minimum(block_e, N_EXPERTS - 1).astype(jnp.int32)
    tok = jnp.broadcast_to(jnp.arange(t, dtype=jnp.int32)[:, None], dest.shape)
    slot_tok = jnp.zeros((n_blocks * MOE_BM,), jnp.int32).at[dest.reshape(-1)].set(tok.reshape(-1))
    present = counts > 0
    run_of_e = jnp.cumsum(present.astype(jnp.int32)) - 1
    ids = jnp.where(present, jnp.arange(N_EXPERTS, dtype=jnp.int32), N_EXPERTS)
    next_incl = lax.cummin(ids[::-1])[::-1]
    next_excl = jnp.concatenate([next_incl[1:], jnp.full((1,), N_EXPERTS, jnp.int32)])
    next_e = jnp.where(next_excl >= N_EXPERTS, -1, next_excl)
    block_run = run_of_e[block_e].astype(jnp.int32)
    block_next = next_e[block_e].astype(jnp.int32)
    return dest.astype(jnp.int32), slot_tok, block_e, n_active.reshape(1), block_run, block_next


def _combine_kernel(dest_ref, h_ref, w_ref, g_ref, b_ref, ys_ref, o_ref, obf_ref, buf, sem, *, tm):
    i = pl.program_id(0)
    n = pl.num_programs(0)

    def row_copy(blk, slot, r, k):
        src = pl.multiple_of(dest_ref[(blk * tm + r) * TOP_K + k] * ROW_TILES, ROW_TILES)
        dst = pl.multiple_of(r * ROW_TILES, ROW_TILES)
        return pltpu.make_async_copy(ys_ref.at[pl.ds(src, ROW_TILES)],
                                     buf.at[slot, k, pl.ds(dst, ROW_TILES)], sem.at[slot])

    def start_block(blk, slot):
        def body(r, carry):
            for k in range(TOP_K):
                row_copy(blk, slot, r, k).start()
            return carry
        lax.fori_loop(0, tm, body, 0, unroll=4)

    def wait_block(blk, slot):
        def body(r, carry):
            for k in range(TOP_K):
                row_copy(blk, slot, r, k).wait()
            return carry
        lax.fori_loop(0, tm, body, 0, unroll=4)

    @pl.when(i == 0)
    def _():
        start_block(0, 0)

    @pl.when(i + 1 < n)
    def _():
        start_block(i + 1, (i + 1) % 2)

    slot = i % 2
    wait_block(i, slot)
    w = w_ref[...]
    ffn = w[:, 0:1] * _load_token_rows(buf.at[slot, 0], tm)
    for k in range(1, TOP_K):
        ffn = ffn + w[:, k:k + 1] * _load_token_rows(buf.at[slot, k], tm)
    y = _layer_norm_rows(ALPHA * h_ref[...] + ffn, g_ref[...], b_ref[...])
    o_ref[...] = y
    obf_ref[...] = y.astype(BF16)


def _combine_norm(dest, h, top_w, ys, g, b, tm=128):
    t, d = h.shape
    tm = min(tm, t)
    row = pl.BlockSpec((tm, d), lambda i, dref: (i, 0))
    vec = pl.BlockSpec((1, d), lambda i, dref: (0, 0))
    grid_spec = pltpu.PrefetchScalarGridSpec(
        num_scalar_prefetch=1,
        grid=(t // tm,),
        in_specs=[row, pl.BlockSpec((tm, LANE), lambda i, dref: (i, 0)), vec, vec,
                  pl.BlockSpec(memory_space=pl.ANY)],
        out_specs=[row, row],
        scratch_shapes=[pltpu.VMEM((2, TOP_K, tm * ROW_TILES, LANE), F32),
                        pltpu.SemaphoreType.DMA((2,))],
    )
    return pl.pallas_call(
        functools.partial(_combine_kernel, tm=tm),
        grid_spec=grid_spec,
        out_shape=[jax.ShapeDtypeStruct((t, d), F32), jax.ShapeDtypeStruct((t, d), BF16)],
        compiler_params=_cparams(("arbitrary",)),
    )(dest.reshape(-1), h, top_w, g.reshape(1, d), b.reshape(1, d), ys)


def _layer(l, h, hb, p):
    t = h.shape[0]
    w_in_t = p['w_in_t']
    u = _matmul(hb, [(w_in_t, (l,), 0)], D_MODEL, F32, lambda acc: acc, tm=1024, tn=1024,
                w_rows_are_outputs=True)
    proj = _matmul(hb, [(w_in_t, (l,), OFF_QKV // 1024)], OFF_BETA - OFF_QKV, BF16, lambda acc: acc,
                   tm=1024, tn=1024, w_rows_are_outputs=True)
    gates = _matmul(hb, [(w_in_t[l, OFF_GATE_S5:], (), 0)], 2 * D_MODEL, BF16,
                    lambda acc: _sigmoid(acc), tm=1024, tn=1024, w_rows_are_outputs=True)
    bd = _matmul(hb, [(w_in_t[l, OFF_BETA:OFF_GATE_S5], (), 0)], 2 * DN_V_HEADS, F32,
                 lambda acc: acc, tn=2 * DN_V_HEADS, w_rows_are_outputs=True)

    y = _s5_apply(u, p['s5_tables'], l)
    part = _matmul(y, [(p['w_glu_a'], (l,), 0), (p['w_glu_b'], (l,), 0)], D_MODEL, BF16,
                   lambda a, b, g: a * _sigmoid(b) * g.astype(F32), extras=[(gates, 0)])

    qkv = _dn_conv(proj, p['dn_conv_w'][l].T)
    bg = _dn_gates(bd, p['dn_a_log'][l], p['dn_dt_bias'][l])
    gc_rows = bg[:, DN_V_HEADS:].T.reshape(DN_V_HEADS, t // DN_C, 1, DN_C)
    o = _delta_rule(qkv, proj, bg, gc_rows, p['dn_norm_w'][l])
    merged = _matmul(o, [(p['w_dn_out'], (l,), 0)], D_MODEL, BF16,
                     lambda acc, g, s: acc * g.astype(F32) + s.astype(F32),
                     extras=[(gates, D_MODEL // 512), (part, 0)])
    mix = _matmul(merged, [(p['w_mix_out'], (l,), 0)], D_MODEL, F32, lambda acc: acc, tm=1024, tn=1024)
    h, hb, h_rows = _deepnorm(h, mix, p['ln1_g'][l], p['ln1_b'][l])

    top_idx, top_w = _router(h, p['w_router'][l], p['b_router'][l])
    dest, slot_tok, block_e, n_active, block_run, block_next = _routing(top_idx[:, :TOP_K])
    ys = _experts(h_rows, slot_tok, block_e, n_active, block_run, block_next, l, p['w_gate'],
                  p['b_gate'], p['w_up'], p['b_up'], p['w_down'], p['b_down'])
    return _combine_norm(dest, h, top_w, ys, p['ln2_g'][l], p['ln2_b'][l])


def kernel(x, w_in, dn_conv_w, dn_a_log, dn_dt_bias, dn_norm_w, w_dn_out, s5_lam_re, s5_lam_im, s5_log_dt, s5_b_re, s5_b_im, s5_c_re, s5_c_im, s5_d, w_glu_a, w_glu_b, w_mix_out, ln1_g, ln1_b, w_router, b_router, w_gate, b_gate, w_up, b_up, w_down, b_down, ln2_g, ln2_b):
    p = dict(w_in_t=jnp.swapaxes(w_in, 1, 2), dn_conv_w=dn_conv_w, dn_a_log=dn_a_log, dn_dt_bias=dn_dt_bias,
             dn_norm_w=dn_norm_w, w_dn_out=w_dn_out, s5_lam_re=s5_lam_re, s5_lam_im=s5_lam_im,
             s5_log_dt=s5_log_dt, s5_b_re=s5_b_re, s5_b_im=s5_b_im, s5_c_re=s5_c_re,
             s5_c_im=s5_c_im, s5_d=s5_d, w_glu_a=w_glu_a, w_glu_b=w_glu_b, w_mix_out=w_mix_out,
             ln1_g=ln1_g, ln1_b=ln1_b, w_router=w_router, b_router=b_router, w_gate=w_gate,
             b_gate=b_gate, w_up=w_up, b_up=b_up, w_down=w_down, b_down=b_down,
             ln2_g=ln2_g, ln2_b=ln2_b)
    bsz, t, d = x.shape
    h = x.reshape(bsz * t, d)
    hb = h.astype(BF16)

    def groups(a):
        return a.reshape((-1,) + a.shape[2:])

    p['s5_tables'] = _s5_tables(groups(s5_lam_re), groups(s5_lam_im), groups(s5_log_dt), groups(s5_b_re),
                                groups(s5_b_im), groups(s5_c_re), groups(s5_c_im), groups(s5_d),
                                bsz * t // S5_L)
    for l in range(w_in.shape[0]):
        h, hb = _layer(l, h, hb, p)
    return h.reshape(bsz, t, d)
```

```python
import functools
import math

import jax
import jax.numpy as jnp
from jax import lax
from jax.experimental import pallas as pl
from jax.experimental.pallas import tpu as pltpu

F32 = jnp.float32
BF16 = jnp.bfloat16

D_MODEL = 2048
DEPTH = 4
S5_P = 16
S5_N = 64
S5_G = D_MODEL // S5_P
S5_L = 16
DN_QK_HEADS = 16
DN_V_HEADS = 32
DN_HD = 128
DN_KEY_W = DN_QK_HEADS * DN_HD
DN_VAL_W = DN_V_HEADS * DN_HD
DN_CONV_CH = 2 * DN_KEY_W + DN_VAL_W
DN_CONV = 4
DN_C = 128
N_EXPERTS = 32
TOP_K = 4
D_EXPERT = 512
SWIGLU_LIMIT = 7.0
SWIGLU_ALPHA = 1.702
MOE_BM = 256
ALPHA = (2 * DEPTH) ** 0.25
LN_EPS = 1e-5
RMS_EPS = 1e-6
L2_EPS = 1e-6
S5_MAX_REAL = -1e-4
OFF_QKV = D_MODEL
OFF_Z = OFF_QKV + DN_CONV_CH
OFF_BETA = OFF_Z + DN_VAL_W
OFF_GATE_S5 = OFF_BETA + 2 * DN_V_HEADS
IN_COLS = OFF_GATE_S5 + 2 * D_MODEL
LANE = 128
VMEM_LIMIT = 56 * 1024 * 1024


def _cparams(sem):
    return pltpu.CompilerParams(dimension_semantics=sem, vmem_limit_bytes=VMEM_LIMIT)


_NT = (((1,), (1,)), ((), ()))
_NN = (((1,), (0,)), ((), ()))


def _mm_kernel(*refs, nw, nx, epilogue, dims):
    a_ref = refs[0]
    w_refs = refs[1:1 + nw]
    x_refs = refs[1 + nw:1 + nw + nx]
    o_ref = refs[1 + nw + nx]
    wbf_refs = refs[2 + nw + nx:]

    @pl.when(pl.program_id(1) == 0)
    def _():
        for w_ref, wbf_ref in zip(w_refs, wbf_refs):
            wbf_ref[...] = w_ref[...].astype(BF16)

    a = a_ref[...].astype(BF16)
    accs = [lax.dot_general(a, wbf[...], dims, preferred_element_type=F32) for wbf in wbf_refs]
    o_ref[...] = epilogue(*accs, *[x[...] for x in x_refs]).astype(o_ref.dtype)


def _matmul(a, ws, n_out, out_dtype, epilogue, extras=(), tm=512, tn=512, w_rows_are_outputs=False):
    m, k = a.shape
    tm = min(tm, m)
    grid = (n_out // tn, m // tm)
    in_specs = [pl.BlockSpec((tm, k), lambda j, i: (i, 0))]
    args = [a]
    wshape = (tn, k) if w_rows_are_outputs else (k, tn)
    for w, lead, off in ws:
        nlead = len(lead)
        if w_rows_are_outputs:
            imap = lambda j, i, lead=lead, off=off: tuple(lead) + (j + off, 0)
        else:
            imap = lambda j, i, lead=lead, off=off: tuple(lead) + (0, j + off)
        in_specs.append(pl.BlockSpec((None,) * nlead + wshape, imap))
        args.append(w)
    for x, off in extras:
        in_specs.append(pl.BlockSpec((tm, tn), lambda j, i, off=off: (i, j + off)))
        args.append(x)
    return pl.pallas_call(
        functools.partial(_mm_kernel, nw=len(ws), nx=len(extras), epilogue=epilogue,
                          dims=_NT if w_rows_are_outputs else _NN),
        grid=grid,
        in_specs=in_specs,
        out_specs=pl.BlockSpec((tm, tn), lambda j, i: (i, j)),
        out_shape=jax.ShapeDtypeStruct((m, n_out), out_dtype),
        scratch_shapes=[pltpu.VMEM(wshape, BF16) for _ in ws],
        compiler_params=_cparams(("arbitrary", "arbitrary")),
    )(*args)


def _sigmoid(x):
    return 1.0 / (1.0 + jnp.exp(-x))


def _split_bf16(x):
    hi = x.astype(BF16)
    lo = (x - hi.astype(F32)).astype(BF16)
    return hi, lo


def _dot3(a, b, dims=_NN):
    ah, al = _split_bf16(a)
    bh, bl = _split_bf16(b)
    return (lax.dot_general(ah, bh, dims, preferred_element_type=F32)
            + lax.dot_general(ah, bl, dims, preferred_element_type=F32)
            + lax.dot_general(al, bh, dims, preferred_element_type=F32))


ROW_TILES = D_MODEL // (2 * LANE)
U32 = jnp.uint32


def _pack_pair(lo, hi):
    def rounded(x):
        b = lax.bitcast_convert_type(x, U32)
        return b + U32(0x7FFF) + ((b >> 16) & U32(1))
    return (rounded(lo) >> 16) | (rounded(hi) & U32(0xFFFF0000))


def _unpack_pair(u):
    return (lax.bitcast_convert_type(u << 16, F32),
            lax.bitcast_convert_type(u & U32(0xFFFF0000), F32))


def _store_token_rows(ref, y):
    n = y.shape[0]
    half = D_MODEL // 2
    for j in range(ROW_TILES):
        ref[pl.ds(j, n, stride=ROW_TILES), :] = _pack_pair(
            y[:, j * LANE:(j + 1) * LANE], y[:, half + j * LANE:half + (j + 1) * LANE])


def _load_token_rows(ref, n):
    pairs = [_unpack_pair(ref[pl.ds(j, n, stride=ROW_TILES), :]) for j in range(ROW_TILES)]
    return jnp.concatenate([p[0] for p in pairs] + [p[1] for p in pairs], axis=1)


def _layer_norm_rows(x, g, b):
    mu = jnp.mean(x, axis=-1, keepdims=True)
    xc = x - mu
    var = jnp.mean(xc * xc, axis=-1, keepdims=True)
    return xc * lax.rsqrt(var + LN_EPS) * g + b


def _ln_kernel(h_ref, r_ref, g_ref, b_ref, o_ref, obf_ref, orow_ref):
    y = _layer_norm_rows(ALPHA * h_ref[...] + r_ref[...], g_ref[...], b_ref[...])
    o_ref[...] = y
    obf_ref[...] = y.astype(BF16)
    _store_token_rows(orow_ref, y)


def _deepnorm(h, r, g, b, tm=256):
    t, d = h.shape
    tm = min(tm, t)
    row = pl.BlockSpec((tm, d), lambda i: (i, 0))
    vec = pl.BlockSpec((1, d), lambda i: (0, 0))
    return pl.pallas_call(
        _ln_kernel,
        grid=(t // tm,),
        in_specs=[row, row, vec, vec],
        out_specs=[row, row, pl.BlockSpec((tm * ROW_TILES, LANE), lambda i: (i, 0))],
        out_shape=[jax.ShapeDtypeStruct((t, d), F32), jax.ShapeDtypeStruct((t, d), BF16),
                   jax.ShapeDtypeStruct((t * ROW_TILES, LANE), U32)],
        compiler_params=_cparams(("arbitrary",)),
    )(h, r, g.reshape(1, d), b.reshape(1, d))


def _s5_tables(lam_re, lam_im, log_dt, b_re, b_im, c_re, c_im, d_skip, n_chunks):
    L = S5_L
    g = lam_re.shape[0]
    lre = jnp.minimum(lam_re, S5_MAX_REAL)
    lim = lam_im
    dt = jnp.exp(log_dt)[:, None]
    ks = jnp.arange(L + 1, dtype=F32)[:, None, None]
    mag = jnp.exp(lre * dt * ks)
    pr = mag * jnp.cos(lim * dt * ks)
    pi = mag * jnp.sin(lim * dt * ks)
    a_re, a_im = pr[1], pi[1]
    den = lre * lre + lim * lim
    f_re = ((a_re - 1.0) * lre + a_im * lim) / den
    f_im = (a_im * lre - (a_re - 1.0) * lim) / den
    bb_re = f_re[..., None] * b_re - f_im[..., None] * b_im
    bb_im = f_re[..., None] * b_im + f_im[..., None] * b_re
    bb = jnp.concatenate([jnp.transpose(bb_re, (0, 2, 1)), jnp.transpose(bb_im, (0, 2, 1))], axis=-1)
    cc = jnp.concatenate([c_re, c_im], axis=-1)
    qr = jnp.transpose(pr[L - 1::-1], (1, 0, 2))
    qi = jnp.transpose(pi[L - 1::-1], (1, 0, 2))
    p1 = jnp.concatenate([qr, qr], axis=-1)
    p2 = jnp.concatenate([-qi, qi], axis=-1)
    ur = jnp.transpose(pr[1:], (1, 0, 2))
    ui = jnp.transpose(pi[1:], (1, 0, 2))
    q1 = jnp.concatenate([ur, -ur], axis=-1)
    q2 = jnp.concatenate([-ui, -ui], axis=-1)
    nlev = max(1, int(math.log2(n_chunks)))
    mr, mi = pr[L], pi[L]
    m1, m2 = [], []
    for _ in range(nlev):
        m1.append(jnp.concatenate([mr, mr], axis=-1))
        m2.append(jnp.concatenate([-mi, mi], axis=-1))
        mr, mi = mr * mr - mi * mi, 2.0 * mr * mi
    pad = [jnp.zeros_like(m1[0])] * (16 - nlev)
    pw1 = jnp.stack(m1 + pad, axis=1)
    pw2 = jnp.stack(m2 + pad, axis=1)
    dsk = jnp.tile(d_skip, (1, L)).reshape(g, 1, L * S5_P)
    return p1, p2, q1, q2, bb, cc, pw1, pw2, dsk


def _gelu_tanh(y):
    return 0.5 * y * (1.0 + jnp.tanh(0.7978845608028654 * (y + 0.044715 * y * y * y)))


def _rep_rows(x, n):
    r, w = x.shape
    return jnp.broadcast_to(x[:, None, :], (r, n, w)).reshape(r * n, w)


def _tile_rows(x, n):
    r, w = x.shape
    return jnp.broadcast_to(x[None, :, :], (n, r, w)).reshape(n * r, w)


def _s5_kernel(u_ref, p1_ref, p2_ref, q1_ref, q2_ref, bb_ref, cc_ref, pw1_ref, pw2_ref, d_ref,
               o_ref, perm_ref, *, nlev, gb):
    L, P, N = S5_L, S5_P, S5_N
    lp = L * P
    half = (L // 2) * LANE
    nc = u_ref.shape[0] // L

    @pl.when(pl.program_id(0) == 0)
    def _():
        r = lax.broadcasted_iota(jnp.int32, (half, half), 0)
        c = lax.broadcasted_iota(jnp.int32, (half, half), 1)
        dst = ((r >> 4) & 7) * LANE + (r >> 7) * P + (r & 15)
        perm_ref[...] = jnp.where(c == dst, 1.0, 0.0).astype(BF16)

    perm = perm_ref[...]
    v = []
    for th in range(2):
        xs = [u_ref[pl.ds(th * 8 + tl, nc, stride=L), :].astype(BF16) for tl in range(8)]
        v.append(jnp.dot(jnp.concatenate(xs, axis=1), perm, preferred_element_type=F32).astype(BF16))

    row = lax.broadcasted_iota(jnp.int32, (nc, 2 * N), 0)
    lane_n = lax.broadcasted_iota(jnp.int32, (1, 2 * N), 1)
    sign = jnp.where(lane_n < N, 1.0, -1.0)
    rblk = lax.broadcasted_iota(jnp.int32, (lp, lp), 0) >> 4
    cblk = lax.broadcasted_iota(jnp.int32, (lp, lp), 1) >> 4
    z = [[], []]
    for gi in range(gb):
        ug = jnp.concatenate([v[0][:, gi * LANE:(gi + 1) * LANE],
                              v[1][:, gi * LANE:(gi + 1) * LANE]], axis=1)
        bbg = bb_ref[gi]
        ccg = cc_ref[gi]
        bmat = (_rep_rows(p1_ref[gi], P) * _tile_rows(bbg, L)
                + _rep_rows(p2_ref[gi], P) * _tile_rows(pltpu.roll(bbg, N, axis=1), L))
        cmt = (_rep_rows(q1_ref[gi], P) * _tile_rows(ccg, L)
               + _rep_rows(q2_ref[gi], P) * _tile_rows(pltpu.roll(ccg, N, axis=1), L))
        w = _dot3(bmat, _tile_rows(ccg * sign, L), _NT)
        tmat = jnp.zeros((lp, lp), F32)
        for t in range(L):
            shift = (lp - (L - 1 - t) * P) % lp
            tmat = jnp.where((cblk == t) & (rblk <= t), pltpu.roll(w, shift, axis=0), tmat)
        x = jnp.dot(ug, bmat.astype(BF16), preferred_element_type=F32)
        for lev in range(nlev):
            d = 1 << lev
            m1 = pw1_ref[gi, lev:lev + 1, :]
            m2 = pw2_ref[gi, lev:lev + 1, :]
            sh = jnp.where(row >= d, pltpu.roll(x, d, axis=0), 0.0)
            x = x + m1 * sh + m2 * pltpu.roll(sh, N, axis=1)
        sprev = jnp.where(row >= 1, pltpu.roll(x, 1, axis=0), 0.0)
        y = (jnp.dot(ug, tmat.astype(BF16), preferred_element_type=F32)
             + lax.dot_general(sprev.astype(BF16), cmt.astype(BF16), _NT, preferred_element_type=F32)
             + d_ref[gi] * ug.astype(F32))
        yg = _gelu_tanh(y).astype(BF16)
        z[0].append(yg[:, :LANE])
        z[1].append(yg[:, LANE:])
    for th in range(2):
        yp = lax.dot_general(jnp.concatenate(z[th], axis=1), perm, _NT, preferred_element_type=F32)
        for tl in range(8):
            o_ref[pl.ds(th * 8 + tl, nc, stride=L), :] = yp[:, tl * LANE:(tl + 1) * LANE]


def _s5_apply(u, tables, layer):
    t, d = u.shape
    gb = LANE // S5_P
    nc = t // S5_L
    nlev = int(math.log2(nc))
    assert (1 << nlev) == nc and nlev <= 16
    first = layer * (d // LANE)
    tab = pl.BlockSpec((gb, 16, 2 * S5_N), lambda i: (first + i, 0, 0))
    return pl.pallas_call(
        functools.partial(_s5_kernel, nlev=nlev, gb=gb),
        grid=(d // LANE,),
        in_specs=[pl.BlockSpec((t, LANE), lambda i: (0, i))] + [tab] * 8
                 + [pl.BlockSpec((gb, 1, S5_L * S5_P), lambda i: (first + i, 0, 0))],
        out_specs=pl.BlockSpec((t, LANE), lambda i: (0, i)),
        out_shape=jax.ShapeDtypeStruct((t, d), F32),
        scratch_shapes=[pltpu.VMEM((8 * LANE, 8 * LANE), BF16)],
        compiler_params=_cparams(("arbitrary",)),
    )(u, *tables)


def _conv_kernel(x_ref, p_ref, w_ref, o_ref, xs_ref, *, nq, nqk):
    j = pl.program_id(0)
    i = pl.program_id(1)
    tm, tc = x_ref.shape
    prev = p_ref[...].astype(F32)
    xs_ref[0:8, :] = jnp.where(i > 0, prev[8:16], 0.0)
    xs_ref[8:8 + tm, :] = x_ref[...].astype(F32)
    w = w_ref[...]
    acc = xs_ref[8:8 + tm, :] * w[DN_CONV - 1:DN_CONV, :]
    for s in range(1, DN_CONV):
        acc = acc + xs_ref[8 - s:8 - s + tm, :] * w[DN_CONV - 1 - s:DN_CONV - s, :]
    y = acc * _sigmoid(acc)
    @pl.when(j < nqk)
    def _():
        qscale = jnp.where(j < nq, DN_HD ** -0.5, 1.0)
        for hh in range(tc // DN_HD):
            blk = y[:, hh * DN_HD:(hh + 1) * DN_HD]
            ss = jnp.sum(blk * blk, axis=-1, keepdims=True)
            o_ref[:, hh * DN_HD:(hh + 1) * DN_HD] = (blk * (lax.rsqrt(ss + L2_EPS) * qscale)).astype(o_ref.dtype)

    @pl.when(j >= nqk)
    def _():
        o_ref[...] = y.astype(o_ref.dtype)


def _dn_conv(proj, conv_wt, tm=512, tc=512):
    t = proj.shape[0]
    tm = min(tm, t)
    pb = tm // 16
    return pl.pallas_call(
        functools.partial(_conv_kernel, nq=DN_KEY_W // tc, nqk=2 * DN_KEY_W // tc),
        grid=(DN_CONV_CH // tc, t // tm),
        in_specs=[pl.BlockSpec((tm, tc), lambda j, i: (i, j)),
                  pl.BlockSpec((16, tc), lambda j, i: (jnp.maximum(i * pb - 1, 0), j)),
                  pl.BlockSpec((DN_CONV, tc), lambda j, i: (0, j))],
        out_specs=pl.BlockSpec((tm, tc), lambda j, i: (i, j)),
        out_shape=jax.ShapeDtypeStruct((t, DN_CONV_CH), BF16),
        scratch_shapes=[pltpu.VMEM((tm + 8, tc), F32)],
        compiler_params=_cparams(("arbitrary", "arbitrary")),
    )(proj, proj, conv_wt)


def _dn_gate_kernel(x_ref, a_ref, dtb_ref, o_ref):
    x = x_ref[...]
    tm, w = x.shape
    lane = lax.broadcasted_iota(jnp.int32, (tm, w), 1)
    row = lax.broadcasted_iota(jnp.int32, (tm, w), 0)
    xs = x + dtb_ref[...]
    softplus = jnp.maximum(xs, 0.0) + jnp.log(1.0 + jnp.exp(-jnp.abs(xs)))
    g = -jnp.exp(a_ref[...]) * softplus
    pos = row & (DN_C - 1)
    d = 1
    while d < DN_C:
        g = g + jnp.where(pos >= d, pltpu.roll(g, d, axis=0), 0.0)
        d *= 2
    o_ref[...] = jnp.where(lane < DN_V_HEADS, _sigmoid(x), g)


def _dn_gates(bd, a_log, dt_bias, tm=512):
    t = bd.shape[0]
    tm = min(tm, t)
    zeros = jnp.zeros((DN_V_HEADS,), F32)
    a2 = jnp.concatenate([zeros, a_log]).reshape(1, -1)
    b2 = jnp.concatenate([zeros, dt_bias]).reshape(1, -1)
    w = 2 * DN_V_HEADS
    return pl.pallas_call(
        _dn_gate_kernel,
        grid=(t // tm,),
        in_specs=[pl.BlockSpec((tm, w), lambda i: (i, 0)),
                  pl.BlockSpec((1, w), lambda i: (0, 0)),
                  pl.BlockSpec((1, w), lambda i: (0, 0))],
        out_specs=pl.BlockSpec((tm, w), lambda i: (i, 0)),
        out_shape=jax.ShapeDtypeStruct((t, w), F32),
        compiler_params=_cparams(("arbitrary",)),
    )(bd, a2, b2)


def _bdot(a, b):
    return jnp.einsum('bij,bjk->bik', a.astype(BF16), b.astype(BF16), preferred_element_type=F32)


def _unit_lower_inverse(a, ii, jj):
    c = a.shape[-1]
    eye = (ii == jj).astype(F32)
    p = jnp.where((ii >> 3) == (jj >> 3), -a, 0.0)
    p2 = _bdot(p, p)
    t = eye + p
    t = t + _bdot(t, p2)
    p4 = _bdot(p2, p2)
    t = t + _bdot(t, p4)
    s = 8
    sh = 3
    while s < c:
        bi = ii >> sh
        bj = jj >> sh
        off = jnp.where(((bi & 1) == 1) & (bj == bi - 1), a, 0.0)
        t = t - _bdot(_bdot(t, off), t)
        s *= 2
        sh += 1
    return t


def _delta_kernel(q_ref, k_ref, v_ref, z_ref, bg_ref, gr_ref, nw_ref, o_ref, s_ref, *, nchunk, nh):
    pp = pl.program_id(0)
    i = pl.program_id(1)

    @pl.when(i == 0)
    def _():
        s_ref[...] = jnp.zeros_like(s_ref)

    c = DN_C
    hd = DN_HD
    bg = bg_ref[...]
    lane = lax.broadcasted_iota(jnp.int32, bg.shape, 1)
    ii = lax.broadcasted_iota(jnp.int32, (c, c), 0)
    jj = lax.broadcasted_iota(jnp.int32, (c, c), 1)
    nw = nw_ref[...]

    q3 = q_ref[...].reshape(nchunk, c, hd)
    k3 = k_ref[...].reshape(nchunk, c, hd)
    kk = jnp.einsum('cid,cjd->cij', k3, k3, preferred_element_type=F32)
    qk = jnp.einsum('cid,cjd->cij', q3, k3, preferred_element_type=F32)
    kf = k3.astype(F32)
    qf = q3.astype(F32)
    a_l, attn_l, rhs_l, qd_l, kd_l, gl_l = [], [], [], [], [], []
    for j in range(nh):
        hh = nh * pp + j
        beta = jnp.sum(jnp.where(lane == hh, bg, 0.0), axis=1, keepdims=True).reshape(nchunk, c, 1)
        gc = jnp.sum(jnp.where(lane == hh + DN_V_HEADS, bg, 0.0), axis=1,
                     keepdims=True).reshape(nchunk, c, 1)
        gr = gr_ref[j]
        decay = jnp.exp(jnp.where(ii >= jj, gc - gr, -jnp.inf))
        a_l.append(jnp.where(ii > jj, kk * decay, 0.0) * beta)
        attn_l.append((qk * decay).astype(BF16))
        eg = jnp.exp(gc)
        v = v_ref[:, j * hd:(j + 1) * hd].astype(F32).reshape(nchunk, c, hd)
        rhs_l.append(jnp.concatenate([v * beta, kf * (beta * eg)], axis=-1))
        qd_l.append((qf * eg).astype(BF16))
        g_last = gr[:, :, c - 1:c]
        kd_l.append(kf * jnp.exp(g_last - gc))
        gl_l.append(jnp.exp(g_last))
    tinv = _unit_lower_inverse(jnp.concatenate(a_l, axis=0), ii, jj)
    sol = _bdot(tinv, jnp.concatenate(rhs_l, axis=0))

    s = [s_ref[j] for j in range(nh)]
    for ci in range(nchunk):
        r0 = ci * c
        for j in range(nh):
            b = j * nchunk + ci
            u_c = sol[b, :, :hd]
            w_c = sol[b, :, hd:]
            sb = s[j].astype(BF16)
            v_new = u_c - jnp.dot(w_c.astype(BF16), sb, preferred_element_type=F32)
            vb = v_new.astype(BF16)
            o = (jnp.dot(qd_l[j][ci], sb, preferred_element_type=F32)
                 + jnp.dot(attn_l[j][ci], vb, preferred_element_type=F32))
            s[j] = s[j] * gl_l[j][ci] + jnp.dot(kd_l[j][ci].T.astype(BF16), vb,
                                               preferred_element_type=F32)
            ms = jnp.mean(o * o, axis=-1, keepdims=True)
            z = z_ref[r0:r0 + c, j * hd:(j + 1) * hd].astype(F32)
            o = o * lax.rsqrt(ms + RMS_EPS) * nw * (z * _sigmoid(z))
            o_ref[r0:r0 + c, j * hd:(j + 1) * hd] = o.astype(o_ref.dtype)
    for j in range(nh):
        s_ref[j] = s[j]


def _delta_rule(qkv, proj, bg, gc_rows, norm_w, rb=1024):
    t = qkv.shape[0]
    rb = min(rb, t)
    nchunk = rb // DN_C
    nh = DN_V_HEADS // DN_QK_HEADS
    kq = DN_KEY_W // DN_HD
    voff = 2 * DN_KEY_W // (nh * DN_HD)
    zoff = DN_CONV_CH // (nh * DN_HD)
    return pl.pallas_call(
        functools.partial(_delta_kernel, nchunk=nchunk, nh=nh),
        grid=(DN_QK_HEADS, t // rb),
        in_specs=[pl.BlockSpec((rb, DN_HD), lambda p, i: (i, p)),
                  pl.BlockSpec((rb, DN_HD), lambda p, i: (i, kq + p)),
                  pl.BlockSpec((rb, nh * DN_HD), lambda p, i: (i, voff + p)),
                  pl.BlockSpec((rb, nh * DN_HD), lambda p, i: (i, zoff + p)),
                  pl.BlockSpec((rb, 2 * DN_V_HEADS), lambda p, i: (i, 0)),
                  pl.BlockSpec((nh, nchunk, 1, DN_C), lambda p, i: (p, i, 0, 0)),
                  pl.BlockSpec((1, DN_HD), lambda p, i: (0, 0))],
        out_specs=pl.BlockSpec((rb, nh * DN_HD), lambda p, i: (i, p)),
        out_shape=jax.ShapeDtypeStruct((t, DN_VAL_W), BF16),
        scratch_shapes=[pltpu.VMEM((nh, DN_HD, DN_HD), F32)],
        compiler_params=_cparams(("arbitrary", "arbitrary")),
    )(qkv, qkv, qkv, proj, bg, gc_rows, norm_w.reshape(1, DN_HD))


def _router_kernel(h_ref, w_ref, b_ref, idx_ref, wt_ref):
    logits = _dot3(h_ref[...], w_ref[...]) + b_ref[...]
    tm, e = logits.shape
    lane = lax.broadcasted_iota(jnp.int32, (tm, e), 1).astype(F32)
    lane_o = lax.broadcasted_iota(jnp.int32, (tm, LANE), 1)
    idx_out = jnp.zeros((tm, LANE), jnp.int32)
    val_out = jnp.zeros((tm, LANE), F32)
    cur = logits
    vals = []
    for kth in range(TOP_K):
        m = jnp.max(cur, axis=-1, keepdims=True)
        sel = jnp.min(jnp.where(cur == m, lane, float(e)), axis=-1, keepdims=True)
        cur = jnp.where(lane == sel, -jnp.inf, cur)
        idx_out = jnp.where(lane_o == kth, sel.astype(jnp.int32), idx_out)
        vals.append(m)
    es = [jnp.exp(v - vals[0]) for v in vals]
    tot = es[0]
    for x in es[1:]:
        tot = tot + x
    for kth in range(TOP_K):
        val_out = jnp.where(lane_o == kth, es[kth] / tot, val_out)
    idx_ref[...] = idx_out
    wt_ref[...] = val_out


def _router(h, w_router, b_router, tm=512):
    t, d = h.shape
    tm = min(tm, t)
    e = w_router.shape[1]
    return pl.pallas_call(
        _router_kernel,
        grid=(t // tm,),
        in_specs=[pl.BlockSpec((tm, d), lambda i: (i, 0)),
                  pl.BlockSpec((d, e), lambda i: (0, 0)),
                  pl.BlockSpec((1, e), lambda i: (0, 0))],
        out_specs=[pl.BlockSpec((tm, LANE), lambda i: (i, 0)),
                   pl.BlockSpec((tm, LANE), lambda i: (i, 0))],
        out_shape=[jax.ShapeDtypeStruct((t, LANE), jnp.int32),
                   jax.ShapeDtypeStruct((t, LANE), F32)],
        compiler_params=_cparams(("arbitrary",)),
    )(h, w_router, b_router.reshape(1, e))


def _expert_kernel(be_ref, na_ref, tok_ref, brun_ref, bnext_ref, h_ref, wg_ref, wu_ref, wd_ref,
                   bg_ref, bu_ref, bd_ref, o_ref, wgb_ref, wub_ref, wdb_ref, xbuf, sem,
                   wgf_ref, wuf_ref, wdf_ref, wsem, *, layer):
    i = pl.program_id(0)
    na = na_ref[0]
    bm = xbuf.shape[1] // ROW_TILES

    def weight_copies(e, s):
        return [pltpu.make_async_copy(src.at[layer, e], dst.at[s], wsem.at[s])
                for src, dst in ((wg_ref, wgf_ref), (wu_ref, wuf_ref), (wd_ref, wdf_ref))]

    @pl.when(i == 0)
    def _():
        for c in weight_copies(be_ref[0], 0):
            c.start()

    def row_copy(blk, slot, r):
        src = pl.multiple_of(tok_ref[blk * bm + r] * ROW_TILES, ROW_TILES)
        dst = pl.multiple_of(r * ROW_TILES, ROW_TILES)
        return pltpu.make_async_copy(h_ref.at[pl.ds(src, ROW_TILES)],
                                     xbuf.at[slot, pl.ds(dst, ROW_TILES)], sem.at[slot])

    def start_block(blk, slot):
        def body(r, carry):
            row_copy(blk, slot, r).start()
            return carry
        lax.fori_loop(0, bm, body, 0, unroll=8)

    nslot = xbuf.shape[0]
    ahead = nslot - 1

    @pl.when(i == 0)
    def _():
        start_block(0, 0)
        for b in range(1, ahead):
            @pl.when(b < na)
            def _():
                start_block(b, b)

    @pl.when(i + ahead < na)
    def _():
        start_block(i + ahead, (i + ahead) % nslot)

    prev = be_ref[jnp.maximum(i - 1, 0)]
    changed = jnp.logical_or(i == 0, be_ref[i] != prev)

    @pl.when(changed)
    def _():
        ws = brun_ref[i] % 2
        for c in weight_copies(be_ref[i], ws):
            c.wait()
        wgb_ref[...] = wgf_ref[ws].astype(BF16)
        wub_ref[...] = wuf_ref[ws].astype(BF16)
        wdb_ref[...] = wdf_ref[ws].astype(BF16)
        nxt = bnext_ref[i]

        @pl.when(nxt >= 0)
        def _():
            for c in weight_copies(nxt, 1 - ws):
                c.start()

    @pl.when(i < na)
    def _():
        slot = i % nslot

        def wait_body(r, carry):
            row_copy(i, slot, r).wait()
            return carry
        lax.fori_loop(0, bm, wait_body, 0, unroll=8)
        x = _load_token_rows(xbuf.at[slot], bm).astype(BF16)
        gate = jnp.minimum(jnp.dot(x, wgb_ref[...], preferred_element_type=F32) + bg_ref[...],
                           SWIGLU_LIMIT)
        up = jnp.clip(jnp.dot(x, wub_ref[...], preferred_element_type=F32) + bu_ref[...],
                      -SWIGLU_LIMIT, SWIGLU_LIMIT)
        hid = gate * _sigmoid(SWIGLU_ALPHA * gate) * (up + 1.0)
        y = jnp.dot(hid.astype(BF16), wdb_ref[...], preferred_element_type=F32) + bd_ref[...]
        _store_token_rows(o_ref, y)

    @pl.when(i >= na)
    def _():
        o_ref[...] = jnp.zeros_like(o_ref)


def _experts(h_rows, slot_tok, block_e, n_active, block_run, block_next, layer,
             w_gate, b_gate, w_up, b_up, w_down, b_down):
    d = D_MODEL
    n_slots = slot_tok.shape[0]
    nb = n_slots // MOE_BM
    f = w_gate.shape[-1]
    l = layer

    def bspec(shape):
        return pl.BlockSpec((None, None) + shape, lambda i, be, *_: (l, be[i], 0, 0))

    hbm = pl.BlockSpec(memory_space=pl.ANY)
    grid_spec = pltpu.PrefetchScalarGridSpec(
        num_scalar_prefetch=5,
        grid=(nb,),
        in_specs=[hbm, hbm, hbm, hbm, bspec((1, f)), bspec((1, f)), bspec((1, d))],
        out_specs=pl.BlockSpec((MOE_BM * ROW_TILES, LANE), lambda i, *_: (i, 0)),
        scratch_shapes=[pltpu.VMEM((d, f), BF16), pltpu.VMEM((d, f), BF16), pltpu.VMEM((f, d), BF16),
                        pltpu.VMEM((3, MOE_BM * ROW_TILES, LANE), U32), pltpu.SemaphoreType.DMA((3,)),
                        pltpu.VMEM((2, d, f), F32), pltpu.VMEM((2, d, f), F32), pltpu.VMEM((2, f, d), F32),
                        pltpu.SemaphoreType.DMA((2,))],
    )
    nl, ne = b_gate.shape[:2]
    return pl.pallas_call(
        functools.partial(_expert_kernel, layer=l),
        grid_spec=grid_spec,
        out_shape=jax.ShapeDtypeStruct((n_slots * ROW_TILES, LANE), U32),
        compiler_params=_cparams(("arbitrary",)),
    )(block_e, n_active, slot_tok, block_run, block_next, h_rows, w_gate, w_up, w_down,
      b_gate.reshape(nl, ne, 1, f), b_up.reshape(nl, ne, 1, f), b_down.reshape(nl, ne, 1, d))


def _routing(top_idx):
    t = top_idx.shape[0]
    sel = jnp.sum(jax.nn.one_hot(top_idx, N_EXPERTS, dtype=jnp.int32), axis=1)
    counts = jnp.sum(sel, axis=0)
    before = jnp.cumsum(sel, axis=0) - sel
    padded = (counts + MOE_BM - 1) // MOE_BM * MOE_BM
    pend = jnp.cumsum(padded)
    pstart = pend - padded
    dest = pstart[top_idx] + jnp.take_along_axis(before, top_idx, axis=1)
    n_blocks = t * TOP_K // MOE_BM + N_EXPERTS
    n_active = (pend[-1] // MOE_BM).astype(jnp.int32)
    blk = jnp.arange(n_blocks, dtype=jnp.int32)
    blk = jnp.minimum(blk, jnp.maximum(n_active - 1, 0))
    block_e = jnp.sum((blk[:, None] * MOE_BM >= pend[None, :]).astype(jnp.int32), axis=1)
    block_e = jnp.minimum(block_e, N_EXPERTS - 1).astype(jnp.int32)
    tok = jnp.broadcast_to(jnp.arange(t, dtype=jnp.int32)[:, None], dest.shape)
    slot_tok = jnp.zeros((n_blocks * MOE_BM,), jnp.int32).at[dest.reshape(-1)].set(tok.reshape(-1))
    present = counts > 0
    run_of_e = jnp.cumsum(present.astype(jnp.int32)) - 1
    ids = jnp.where(present, jnp.arange(N_EXPERTS, dtype=jnp.int32), N_EXPERTS)
    next_incl = lax.cummin(ids[::-1])[::-1]
    next_excl = jnp.concatenate([next_incl[1:], jnp.full((1,), N_EXPERTS, jnp.int32)])
    next_e = jnp.where(next_excl >= N_EXPERTS, -1, next_excl)
    block_run = run_of_e[block_e].astype(jnp.int32)
    block_next = next_e[block_e].astype(jnp.int32)
    return dest.astype(jnp.int32), slot_tok, block_e, n_active.reshape(1), block_run, block_next


def _combine_kernel(dest_ref, h_ref, w_ref, g_ref, b_ref, ys_ref, o_ref, obf_ref, buf, sem, *, tm):
    i = pl.program_id(0)
    n = pl.num_programs(0)

    def row_copy(blk, slot, r, k):
        src = pl.multiple_of(dest_ref[(blk * tm + r) * TOP_K + k] * ROW_TILES, ROW_TILES)
        dst = pl.multiple_of(r * ROW_TILES, ROW_TILES)
        return pltpu.make_async_copy(ys_ref.at[pl.ds(src, ROW_TILES)],
                                     buf.at[slot, k, pl.ds(dst, ROW_TILES)], sem.at[slot])

    def start_block(blk, slot):
        def body(r, carry):
            for k in range(TOP_K):
                row_copy(blk, slot, r, k).start()
            return carry
        lax.fori_loop(0, tm, body, 0, unroll=4)

    def wait_block(blk, slot):
        def body(r, carry):
            for k in range(TOP_K):
                row_copy(blk, slot, r, k).wait()
            return carry
        lax.fori_loop(0, tm, body, 0, unroll=4)

    nslot = buf.shape[0]
    ahead = nslot - 1

    @pl.when(i == 0)
    def _():
        start_block(0, 0)
        for b in range(1, ahead):
            @pl.when(b < n)
            def _():
                start_block(b, b)

    @pl.when(i + ahead < n)
    def _():
        start_block(i + ahead, (i + ahead) % nslot)

    slot = i % nslot
    wait_block(i, slot)
    w = w_ref[...]
    ffn = w[:, 0:1] * _load_token_rows(buf.at[slot, 0], tm)
    for k in range(1, TOP_K):
        ffn = ffn + w[:, k:k + 1] * _load_token_rows(buf.at[slot, k], tm)
    y = _layer_norm_rows(ALPHA * h_ref[...] + ffn, g_ref[...], b_ref[...])
    o_ref[...] = y
    obf_ref[...] = y.astype(BF16)


def _combine_norm(dest, h, top_w, ys, g, b, tm=128):
    t, d = h.shape
    tm = min(tm, t)
    row = pl.BlockSpec((tm, d), lambda i, dref: (i, 0))
    vec = pl.BlockSpec((1, d), lambda i, dref: (0, 0))
    grid_spec = pltpu.PrefetchScalarGridSpec(
        num_scalar_prefetch=1,
        grid=(t // tm,),
        in_specs=[row, pl.BlockSpec((tm, LANE), lambda i, dref: (i, 0)), vec, vec,
                  pl.BlockSpec(memory_space=pl.ANY)],
        out_specs=[row, row],
        scratch_shapes=[pltpu.VMEM((3, TOP_K, tm * ROW_TILES, LANE), U32),
                        pltpu.SemaphoreType.DMA((3,))],
    )
    return pl.pallas_call(
        functools.partial(_combine_kernel, tm=tm),
        grid_spec=grid_spec,
        out_shape=[jax.ShapeDtypeStruct((t, d), F32), jax.ShapeDtypeStruct((t, d), BF16)],
        compiler_params=_cparams(("arbitrary",)),
    )(dest.reshape(-1), h, top_w, g.reshape(1, d), b.reshape(1, d), ys)


def _layer(l, h, hb, p):
    t = h.shape[0]
    w_in_t = p['w_in_t']
    u = _matmul(hb, [(w_in_t, (l,), 0)], D_MODEL, F32, lambda acc: acc, tm=1024, tn=1024,
                w_rows_are_outputs=True)
    proj = _matmul(hb, [(w_in_t, (l,), OFF_QKV // 1024)], OFF_BETA - OFF_QKV, BF16, lambda acc: acc,
                   tm=1024, tn=1024, w_rows_are_outputs=True)
    gates = _matmul(hb, [(w_in_t[l, OFF_GATE_S5:], (), 0)], 2 * D_MODEL, BF16,
                    lambda acc: _sigmoid(acc), tm=1024, tn=1024, w_rows_are_outputs=True)
    bd = _matmul(hb, [(w_in_t[l, OFF_BETA:OFF_GATE_S5], (), 0)], 2 * DN_V_HEADS, F32,
                 lambda acc: acc, tn=2 * DN_V_HEADS, w_rows_are_outputs=True)

    y = _s5_apply(u, p['s5_tables'], l)
    part = _matmul(y, [(p['w_glu_a'], (l,), 0), (p['w_glu_b'], (l,), 0)], D_MODEL, BF16,
                   lambda a, b, g: a * _sigmoid(b) * g.astype(F32), extras=[(gates, 0)])

    qkv = _dn_conv(proj, p['dn_conv_w'][l].T)
    bg = _dn_gates(bd, p['dn_a_log'][l], p['dn_dt_bias'][l])
    gc_rows = bg[:, DN_V_HEADS:].T.reshape(DN_V_HEADS, t // DN_C, 1, DN_C)
    o = _delta_rule(qkv, proj, bg, gc_rows, p['dn_norm_w'][l])
    merged = _matmul(o, [(p['w_dn_out'], (l,), 0)], D_MODEL, BF16,
                     lambda acc, g, s: acc * g.astype(F32) + s.astype(F32),
                     extras=[(gates, D_MODEL // 512), (part, 0)])
    mix = _matmul(merged, [(p['w_mix_out'], (l,), 0)], D_MODEL, F32, lambda acc: acc, tm=1024, tn=1024)
    h, hb, h_rows = _deepnorm(h, mix, p['ln1_g'][l], p['ln1_b'][l])

    top_idx, top_w = _router(h, p['w_router'][l], p['b_router'][l])
    dest, slot_tok, block_e, n_active, block_run, block_next = _routing(top_idx[:, :TOP_K])
    ys = _experts(h_rows, slot_tok, block_e, n_active, block_run, block_next, l, p['w_gate'],
                  p['b_gate'], p['w_up'], p['b_up'], p['w_down'], p['b_down'])
    return _combine_norm(dest, h, top_w, ys, p['ln2_g'][l], p['ln2_b'][l])


def kernel(x, w_in, dn_conv_w, dn_a_log, dn_dt_bias, dn_norm_w, w_dn_out, s5_lam_re, s5_lam_im, s5_log_dt, s5_b_re, s5_b_im, s5_c_re, s5_c_im, s5_d, w_glu_a, w_glu_b, w_mix_out, ln1_g, ln1_b, w_router, b_router, w_gate, b_gate, w_up, b_up, w_down, b_down, ln2_g, ln2_b):
    p = dict(w_in_t=jnp.swapaxes(w_in, 1, 2), dn_conv_w=dn_conv_w, dn_a_log=dn_a_log, dn_dt_bias=dn_dt_bias,
             dn_norm_w=dn_norm_w, w_dn_out=w_dn_out, s5_lam_re=s5_lam_re, s5_lam_im=s5_lam_im,
             s5_log_dt=s5_log_dt, s5_b_re=s5_b_re, s5_b_im=s5_b_im, s5_c_re=s5_c_re,
             s5_c_im=s5_c_im, s5_d=s5_d, w_glu_a=w_glu_a, w_glu_b=w_glu_b, w_mix_out=w_mix_out,
             ln1_g=ln1_g, ln1_b=ln1_b, w_router=w_router, b_router=b_router, w_gate=w_gate,
             b_gate=b_gate, w_up=w_up, b_up=b_up, w_down=w_down, b_down=b_down,
             ln2_g=ln2_g, ln2_b=ln2_b)
    bsz, t, d = x.shape
    h = x.reshape(bsz * t, d)
    hb = h.astype(BF16)

    def groups(a):
        return a.reshape((-1,) + a.shape[2:])

    p['s5_tables'] = _s5_tables(groups(s5_lam_re), groups(s5_lam_im), groups(s5_log_dt), groups(s5_b_re),
                                groups(s5_b_im), groups(s5_c_re), groups(s5_c_im), groups(s5_d),
                                bsz * t // S5_L)
    for l in range(w_in.shape[0]):
        h, hb = _layer(l, h, hb, p)
    return h.reshape(bsz, t, d)
```

```python
import functools
import math

import jax
import jax.numpy as jnp
from jax import lax
from jax.experimental import pallas as pl
from jax.experimental.pallas import tpu as pltpu

F32 = jnp.float32
BF16 = jnp.bfloat16

D_MODEL = 2048
DEPTH = 4
S5_P = 16
S5_N = 64
S5_G = D_MODEL // S5_P
S5_L = 16
DN_QK_HEADS = 16
DN_V_HEADS = 32
DN_HD = 128
DN_KEY_W = DN_QK_HEADS * DN_HD
DN_VAL_W = DN_V_HEADS * DN_HD
DN_CONV_CH = 2 * DN_KEY_W + DN_VAL_W
DN_CONV = 4
DN_C = 128
N_EXPERTS = 32
TOP_K = 4
D_EXPERT = 512
SWIGLU_LIMIT = 7.0
SWIGLU_ALPHA = 1.702
MOE_BM = 256
ALPHA = (2 * DEPTH) ** 0.25
LN_EPS = 1e-5
RMS_EPS = 1e-6
L2_EPS = 1e-6
S5_MAX_REAL = -1e-4
OFF_QKV = D_MODEL
OFF_Z = OFF_QKV + DN_CONV_CH
OFF_BETA = OFF_Z + DN_VAL_W
OFF_GATE_S5 = OFF_BETA + 2 * DN_V_HEADS
IN_COLS = OFF_GATE_S5 + 2 * D_MODEL
LANE = 128
VMEM_LIMIT = 56 * 1024 * 1024


def _cparams(sem):
    return pltpu.CompilerParams(dimension_semantics=sem, vmem_limit_bytes=VMEM_LIMIT)


_NT = (((1,), (1,)), ((), ()))
_NN = (((1,), (0,)), ((), ()))


def _mm_kernel(*refs, nw, nx, epilogue, dims):
    a_ref = refs[0]
    w_refs = refs[1:1 + nw]
    x_refs = refs[1 + nw:1 + nw + nx]
    o_ref = refs[1 + nw + nx]
    wbf_refs = refs[2 + nw + nx:]

    @pl.when(pl.program_id(1) == 0)
    def _():
        for w_ref, wbf_ref in zip(w_refs, wbf_refs):
            wbf_ref[...] = w_ref[...].astype(BF16)

    a = a_ref[...].astype(BF16)
    accs = [lax.dot_general(a, wbf[...], dims, preferred_element_type=F32) for wbf in wbf_refs]
    o_ref[...] = epilogue(*accs, *[x[...] for x in x_refs]).astype(o_ref.dtype)


def _matmul(a, ws, n_out, out_dtype, epilogue, extras=(), tm=512, tn=512, w_rows_are_outputs=False):
    m, k = a.shape
    tm = min(tm, m)
    grid = (n_out // tn, m // tm)
    in_specs = [pl.BlockSpec((tm, k), lambda j, i: (i, 0))]
    args = [a]
    wshape = (tn, k) if w_rows_are_outputs else (k, tn)
    for w, lead, off in ws:
        nlead = len(lead)
        if w_rows_are_outputs:
            imap = lambda j, i, lead=lead, off=off: tuple(lead) + (j + off, 0)
        else:
            imap = lambda j, i, lead=lead, off=off: tuple(lead) + (0, j + off)
        in_specs.append(pl.BlockSpec((None,) * nlead + wshape, imap))
        args.append(w)
    for x, off in extras:
        in_specs.append(pl.BlockSpec((tm, tn), lambda j, i, off=off: (i, j + off)))
        args.append(x)
    return pl.pallas_call(
        functools.partial(_mm_kernel, nw=len(ws), nx=len(extras), epilogue=epilogue,
                          dims=_NT if w_rows_are_outputs else _NN),
        grid=grid,
        in_specs=in_specs,
        out_specs=pl.BlockSpec((tm, tn), lambda j, i: (i, j)),
        out_shape=jax.ShapeDtypeStruct((m, n_out), out_dtype),
        scratch_shapes=[pltpu.VMEM(wshape, BF16) for _ in ws],
        compiler_params=_cparams(("arbitrary", "arbitrary")),
    )(*args)


def _sigmoid(x):
    return 1.0 / (1.0 + jnp.exp(-x))


def _split_bf16(x):
    hi = x.astype(BF16)
    lo = (x - hi.astype(F32)).astype(BF16)
    return hi, lo


def _dot3(a, b, dims=_NN):
    ah, al = _split_bf16(a)
    bh, bl = _split_bf16(b)
    return (lax.dot_general(ah, bh, dims, preferred_element_type=F32)
            + lax.dot_general(ah, bl, dims, preferred_element_type=F32)
            + lax.dot_general(al, bh, dims, preferred_element_type=F32))


ROW_TILES = D_MODEL // (2 * LANE)
U32 = jnp.uint32


def _pack_pair(lo, hi):
    def rounded(x):
        b = lax.bitcast_convert_type(x, U32)
        return b + U32(0x7FFF) + ((b >> 16) & U32(1))
    return (rounded(lo) >> 16) | (rounded(hi) & U32(0xFFFF0000))


def _unpack_pair(u):
    return (lax.bitcast_convert_type(u << 16, F32),
            lax.bitcast_convert_type(u & U32(0xFFFF0000), F32))


def _store_token_rows(ref, y):
    n = y.shape[0]
    half = D_MODEL // 2
    for j in range(ROW_TILES):
        ref[pl.ds(j, n, stride=ROW_TILES), :] = _pack_pair(
            y[:, j * LANE:(j + 1) * LANE], y[:, half + j * LANE:half + (j + 1) * LANE])


def _load_token_rows(ref, n):
    pairs = [_unpack_pair(ref[pl.ds(j, n, stride=ROW_TILES), :]) for j in range(ROW_TILES)]
    return jnp.concatenate([p[0] for p in pairs] + [p[1] for p in pairs], axis=1)


def _layer_norm_rows(x, g, b):
    mu = jnp.mean(x, axis=-1, keepdims=True)
    xc = x - mu
    var = jnp.mean(xc * xc, axis=-1, keepdims=True)
    return xc * lax.rsqrt(var + LN_EPS) * g + b


def _ln_kernel(h_ref, r_ref, g_ref, b_ref, o_ref, obf_ref, orow_ref):
    y = _layer_norm_rows(ALPHA * h_ref[...] + r_ref[...], g_ref[...], b_ref[...])
    o_ref[...] = y
    obf_ref[...] = y.astype(BF16)
    _store_token_rows(orow_ref, y)


def _deepnorm(h, r, g, b, tm=256):
    t, d = h.shape
    tm = min(tm, t)
    row = pl.BlockSpec((tm, d), lambda i: (i, 0))
    vec = pl.BlockSpec((1, d), lambda i: (0, 0))
    return pl.pallas_call(
        _ln_kernel,
        grid=(t // tm,),
        in_specs=[row, row, vec, vec],
        out_specs=[row, row, pl.BlockSpec((tm * ROW_TILES, LANE), lambda i: (i, 0))],
        out_shape=[jax.ShapeDtypeStruct((t, d), F32), jax.ShapeDtypeStruct((t, d), BF16),
                   jax.ShapeDtypeStruct((t * ROW_TILES, LANE), U32)],
        compiler_params=_cparams(("arbitrary",)),
    )(h, r, g.reshape(1, d), b.reshape(1, d))


def _s5_tables(lam_re, lam_im, log_dt, b_re, b_im, c_re, c_im, d_skip, n_chunks):
    L = S5_L
    g = lam_re.shape[0]
    lre = jnp.minimum(lam_re, S5_MAX_REAL)
    lim = lam_im
    dt = jnp.exp(log_dt)[:, None]
    ks = jnp.arange(L + 1, dtype=F32)[:, None, None]
    mag = jnp.exp(lre * dt * ks)
    pr = mag * jnp.cos(lim * dt * ks)
    pi = mag * jnp.sin(lim * dt * ks)
    a_re, a_im = pr[1], pi[1]
    den = lre * lre + lim * lim
    f_re = ((a_re - 1.0) * lre + a_im * lim) / den
    f_im = (a_im * lre - (a_re - 1.0) * lim) / den
    bb_re = f_re[..., None] * b_re - f_im[..., None] * b_im
    bb_im = f_re[..., None] * b_im + f_im[..., None] * b_re
    bb = jnp.concatenate([jnp.transpose(bb_re, (0, 2, 1)), jnp.transpose(bb_im, (0, 2, 1))], axis=-1)
    cc = jnp.concatenate([c_re, c_im], axis=-1)
    qr = jnp.transpose(pr[L - 1::-1], (1, 0, 2))
    qi = jnp.transpose(pi[L - 1::-1], (1, 0, 2))
    p1 = jnp.concatenate([qr, qr], axis=-1)
    p2 = jnp.concatenate([-qi, qi], axis=-1)
    ur = jnp.transpose(pr[1:], (1, 0, 2))
    ui = jnp.transpose(pi[1:], (1, 0, 2))
    q1 = jnp.concatenate([ur, -ur], axis=-1)
    q2 = jnp.concatenate([-ui, -ui], axis=-1)
    nlev = max(1, int(math.log2(n_chunks)))
    mr, mi = pr[L], pi[L]
    m1, m2 = [], []
    for _ in range(nlev):
        m1.append(jnp.concatenate([mr, mr], axis=-1))
        m2.append(jnp.concatenate([-mi, mi], axis=-1))
        mr, mi = mr * mr - mi * mi, 2.0 * mr * mi
    pad = [jnp.zeros_like(m1[0])] * (16 - nlev)
    pw1 = jnp.stack(m1 + pad, axis=1)
    pw2 = jnp.stack(m2 + pad, axis=1)
    dsk = jnp.tile(d_skip, (1, L)).reshape(g, 1, L * S5_P)
    return p1, p2, q1, q2, bb, cc, pw1, pw2, dsk


def _gelu_tanh(y):
    return 0.5 * y * (1.0 + jnp.tanh(0.7978845608028654 * (y + 0.044715 * y * y * y)))


def _rep_rows(x, n):
    r, w = x.shape
    return jnp.broadcast_to(x[:, None, :], (r, n, w)).reshape(r * n, w)


def _tile_rows(x, n):
    r, w = x.shape
    return jnp.broadcast_to(x[None, :, :], (n, r, w)).reshape(n * r, w)


def _s5_kernel(u_ref, p1_ref, p2_ref, q1_ref, q2_ref, bb_ref, cc_ref, pw1_ref, pw2_ref, d_ref,
               o_ref, perm_ref, *, nlev, gb):
    L, P, N = S5_L, S5_P, S5_N
    lp = L * P
    half = (L // 2) * LANE
    nc = u_ref.shape[0] // L

    @pl.when(pl.program_id(0) == 0)
    def _():
        r = lax.broadcasted_iota(jnp.int32, (half, half), 0)
        c = lax.broadcasted_iota(jnp.int32, (half, half), 1)
        dst = ((r >> 4) & 7) * LANE + (r >> 7) * P + (r & 15)
        perm_ref[...] = jnp.where(c == dst, 1.0, 0.0).astype(BF16)

    perm = perm_ref[...]
    v = []
    for th in range(2):
        xs = [u_ref[pl.ds(th * 8 + tl, nc, stride=L), :].astype(BF16) for tl in range(8)]
        v.append(jnp.dot(jnp.concatenate(xs, axis=1), perm, preferred_element_type=F32).astype(BF16))

    row = lax.broadcasted_iota(jnp.int32, (nc, 2 * N), 0)
    lane_n = lax.broadcasted_iota(jnp.int32, (1, 2 * N), 1)
    sign = jnp.where(lane_n < N, 1.0, -1.0)
    rblk = lax.broadcasted_iota(jnp.int32, (lp, lp), 0) >> 4
    cblk = lax.broadcasted_iota(jnp.int32, (lp, lp), 1) >> 4
    z = [[], []]
    for gi in range(gb):
        ug = jnp.concatenate([v[0][:, gi * LANE:(gi + 1) * LANE],
                              v[1][:, gi * LANE:(gi + 1) * LANE]], axis=1)
        bbg = bb_ref[gi]
        ccg = cc_ref[gi]
        bmat = (_rep_rows(p1_ref[gi], P) * _tile_rows(bbg, L)
                + _rep_rows(p2_ref[gi], P) * _tile_rows(pltpu.roll(bbg, N, axis=1), L))
        cmt = (_rep_rows(q1_ref[gi], P) * _tile_rows(ccg, L)
               + _rep_rows(q2_ref[gi], P) * _tile_rows(pltpu.roll(ccg, N, axis=1), L))
        w = _dot3(bmat, _tile_rows(ccg * sign, L), _NT)
        tmat = jnp.zeros((lp, lp), F32)
        for t in range(L):
            shift = (lp - (L - 1 - t) * P) % lp
            tmat = jnp.where((cblk == t) & (rblk <= t), pltpu.roll(w, shift, axis=0), tmat)
        x = jnp.dot(ug, bmat.astype(BF16), preferred_element_type=F32)
        for lev in range(nlev):
            d = 1 << lev
            m1 = pw1_ref[gi, lev:lev + 1, :]
            m2 = pw2_ref[gi, lev:lev + 1, :]
            sh = jnp.where(row >= d, pltpu.roll(x, d, axis=0), 0.0)
            x = x + m1 * sh + m2 * pltpu.roll(sh, N, axis=1)
        sprev = jnp.where(row >= 1, pltpu.roll(x, 1, axis=0), 0.0)
        y = (jnp.dot(ug, tmat.astype(BF16), preferred_element_type=F32)
             + lax.dot_general(sprev.astype(BF16), cmt.astype(BF16), _NT, preferred_element_type=F32)
             + d_ref[gi] * ug.astype(F32))
        yg = _gelu_tanh(y).astype(BF16)
        z[0].append(yg[:, :LANE])
        z[1].append(yg[:, LANE:])
    for th in range(2):
        yp = lax.dot_general(jnp.concatenate(z[th], axis=1), perm, _NT, preferred_element_type=F32)
        for tl in range(8):
            o_ref[pl.ds(th * 8 + tl, nc, stride=L), :] = yp[:, tl * LANE:(tl + 1) * LANE]


def _s5_apply(u, tables, layer):
    t, d = u.shape
    gb = LANE // S5_P
    nc = t // S5_L
    nlev = int(math.log2(nc))
    assert (1 << nlev) == nc and nlev <= 16
    first = layer * (d // LANE)
    tab = pl.BlockSpec((gb, 16, 2 * S5_N), lambda i: (first + i, 0, 0))
    return pl.pallas_call(
        functools.partial(_s5_kernel, nlev=nlev, gb=gb),
        grid=(d // LANE,),
        in_specs=[pl.BlockSpec((t, LANE), lambda i: (0, i))] + [tab] * 8
                 + [pl.BlockSpec((gb, 1, S5_L * S5_P), lambda i: (first + i, 0, 0))],
        out_specs=pl.BlockSpec((t, LANE), lambda i: (0, i)),
        out_shape=jax.ShapeDtypeStruct((t, d), F32),
        scratch_shapes=[pltpu.VMEM((8 * LANE, 8 * LANE), BF16)],
        compiler_params=_cparams(("arbitrary",)),
    )(u, *tables)


def _conv_kernel(x_ref, p_ref, w_ref, o_ref, xs_ref, *, nq, nqk):
    j = pl.program_id(0)
    i = pl.program_id(1)
    tm, tc = x_ref.shape
    prev = p_ref[...].astype(F32)
    xs_ref[0:8, :] = jnp.where(i > 0, prev[8:16], 0.0)
    xs_ref[8:8 + tm, :] = x_ref[...].astype(F32)
    w = w_ref[...]
    acc = xs_ref[8:8 + tm, :] * w[DN_CONV - 1:DN_CONV, :]
    for s in range(1, DN_CONV):
        acc = acc + xs_ref[8 - s:8 - s + tm, :] * w[DN_CONV - 1 - s:DN_CONV - s, :]
    y = acc * _sigmoid(acc)
    @pl.when(j < nqk)
    def _():
        qscale = jnp.where(j < nq, DN_HD ** -0.5, 1.0)
        for hh in range(tc // DN_HD):
            blk = y[:, hh * DN_HD:(hh + 1) * DN_HD]
            ss = jnp.sum(blk * blk, axis=-1, keepdims=True)
            o_ref[:, hh * DN_HD:(hh + 1) * DN_HD] = (blk * (lax.rsqrt(ss + L2_EPS) * qscale)).astype(o_ref.dtype)

    @pl.when(j >= nqk)
    def _():
        o_ref[...] = y.astype(o_ref.dtype)


def _dn_conv(proj, conv_wt, tm=512, tc=512):
    t = proj.shape[0]
    tm = min(tm, t)
    pb = tm // 16
    return pl.pallas_call(
        functools.partial(_conv_kernel, nq=DN_KEY_W // tc, nqk=2 * DN_KEY_W // tc),
        grid=(DN_CONV_CH // tc, t // tm),
        in_specs=[pl.BlockSpec((tm, tc), lambda j, i: (i, j)),
                  pl.BlockSpec((16, tc), lambda j, i: (jnp.maximum(i * pb - 1, 0), j)),
                  pl.BlockSpec((DN_CONV, tc), lambda j, i: (0, j))],
        out_specs=pl.BlockSpec((tm, tc), lambda j, i: (i, j)),
        out_shape=jax.ShapeDtypeStruct((t, DN_CONV_CH), BF16),
        scratch_shapes=[pltpu.VMEM((tm + 8, tc), F32)],
        compiler_params=_cparams(("arbitrary", "arbitrary")),
    )(proj, proj, conv_wt)


def _dn_gate_kernel(x_ref, a_ref, dtb_ref, o_ref):
    x = x_ref[...]
    tm, w = x.shape
    lane = lax.broadcasted_iota(jnp.int32, (tm, w), 1)
    row = lax.broadcasted_iota(jnp.int32, (tm, w), 0)
    xs = x + dtb_ref[...]
    softplus = jnp.maximum(xs, 0.0) + jnp.log(1.0 + jnp.exp(-jnp.abs(xs)))
    g = -jnp.exp(a_ref[...]) * softplus
    pos = row & (DN_C - 1)
    d = 1
    while d < DN_C:
        g = g + jnp.where(pos >= d, pltpu.roll(g, d, axis=0), 0.0)
        d *= 2
    o_ref[...] = jnp.where(lane < DN_V_HEADS, _sigmoid(x), g)


def _dn_gates(bd, a_log, dt_bias, tm=512):
    t = bd.shape[0]
    tm = min(tm, t)
    zeros = jnp.zeros((DN_V_HEADS,), F32)
    a2 = jnp.concatenate([zeros, a_log]).reshape(1, -1)
    b2 = jnp.concatenate([zeros, dt_bias]).reshape(1, -1)
    w = 2 * DN_V_HEADS
    return pl.pallas_call(
        _dn_gate_kernel,
        grid=(t // tm,),
        in_specs=[pl.BlockSpec((tm, w), lambda i: (i, 0)),
                  pl.BlockSpec((1, w), lambda i: (0, 0)),
                  pl.BlockSpec((1, w), lambda i: (0, 0))],
        out_specs=pl.BlockSpec((tm, w), lambda i: (i, 0)),
        out_shape=jax.ShapeDtypeStruct((t, w), F32),
        compiler_params=_cparams(("arbitrary",)),
    )(bd, a2, b2)


def _bdot(a, b):
    return jnp.einsum('bij,bjk->bik', a.astype(BF16), b.astype(BF16), preferred_element_type=F32)


def _unit_lower_inverse(a, ii, jj):
    c = a.shape[-1]
    eye = (ii == jj).astype(F32)
    p = jnp.where((ii >> 3) == (jj >> 3), -a, 0.0)
    p2 = _bdot(p, p)
    t = eye + p
    t = t + _bdot(t, p2)
    p4 = _bdot(p2, p2)
    t = t + _bdot(t, p4)
    s = 8
    sh = 3
    while s < c:
        bi = ii >> sh
        bj = jj >> sh
        off = jnp.where(((bi & 1) == 1) & (bj == bi - 1), a, 0.0)
        t = t - _bdot(_bdot(t, off), t)
        s *= 2
        sh += 1
    return t


def _delta_kernel(q_ref, k_ref, v_ref, z_ref, bg_ref, gr_ref, nw_ref, o_ref, s_ref, *, nchunk, nh):
    pp = pl.program_id(0)
    i = pl.program_id(1)

    @pl.when(i == 0)
    def _():
        s_ref[...] = jnp.zeros_like(s_ref)

    c = DN_C
    hd = DN_HD
    bg = bg_ref[...]
    lane = lax.broadcasted_iota(jnp.int32, bg.shape, 1)
    ii = lax.broadcasted_iota(jnp.int32, (c, c), 0)
    jj = lax.broadcasted_iota(jnp.int32, (c, c), 1)
    nw = nw_ref[...]

    q3 = q_ref[...].reshape(nchunk, c, hd)
    k3 = k_ref[...].reshape(nchunk, c, hd)
    kk = jnp.einsum('cid,cjd->cij', k3, k3, preferred_element_type=F32)
    qk = jnp.einsum('cid,cjd->cij', q3, k3, preferred_element_type=F32)
    kf = k3.astype(F32)
    qf = q3.astype(F32)
    a_l, attn_l, rhs_l, qd_l, kd_l, gl_l = [], [], [], [], [], []
    for j in range(nh):
        hh = nh * pp + j
        beta = jnp.sum(jnp.where(lane == hh, bg, 0.0), axis=1, keepdims=True).reshape(nchunk, c, 1)
        gc = jnp.sum(jnp.where(lane == hh + DN_V_HEADS, bg, 0.0), axis=1,
                     keepdims=True).reshape(nchunk, c, 1)
        gr = gr_ref[j]
        decay = jnp.exp(jnp.where(ii >= jj, gc - gr, -jnp.inf))
        a_l.append(jnp.where(ii > jj, kk * decay, 0.0) * beta)
        attn_l.append((qk * decay).astype(BF16))
        eg = jnp.exp(gc)
        v = v_ref[:, j * hd:(j + 1) * hd].astype(F32).reshape(nchunk, c, hd)
        rhs_l.append(jnp.concatenate([v * beta, kf * (beta * eg)], axis=-1))
        qd_l.append((qf * eg).astype(BF16))
        g_last = gr[:, :, c - 1:c]
        kd_l.append(kf * jnp.exp(g_last - gc))
        gl_l.append(jnp.exp(g_last))
    tinv = _unit_lower_inverse(jnp.concatenate(a_l, axis=0), ii, jj)
    sol = _bdot(tinv, jnp.concatenate(rhs_l, axis=0))

    s = [s_ref[j] for j in range(nh)]
    for ci in range(nchunk):
        r0 = ci * c
        for j in range(nh):
            b = j * nchunk + ci
            u_c = sol[b, :, :hd]
            w_c = sol[b, :, hd:]
            sb = s[j].astype(BF16)
            v_new = u_c - jnp.dot(w_c.astype(BF16), sb, preferred_element_type=F32)
            vb = v_new.astype(BF16)
            o = (jnp.dot(qd_l[j][ci], sb, preferred_element_type=F32)
                 + jnp.dot(attn_l[j][ci], vb, preferred_element_type=F32))
            s[j] = s[j] * gl_l[j][ci] + jnp.dot(kd_l[j][ci].T.astype(BF16), vb,
                                               preferred_element_type=F32)
            ms = jnp.mean(o * o, axis=-1, keepdims=True)
            z = z_ref[r0:r0 + c, j * hd:(j + 1) * hd].astype(F32)
            o = o * lax.rsqrt(ms + RMS_EPS) * nw * (z * _sigmoid(z))
            o_ref[r0:r0 + c, j * hd:(j + 1) * hd] = o.astype(o_ref.dtype)
    for j in range(nh):
        s_ref[j] = s[j]


def _delta_rule(qkv, proj, bg, gc_rows, norm_w, rb=1024):
    t = qkv.shape[0]
    rb = min(rb, t)
    nchunk = rb // DN_C
    nh = DN_V_HEADS // DN_QK_HEADS
    kq = DN_KEY_W // DN_HD
    voff = 2 * DN_KEY_W // (nh * DN_HD)
    zoff = DN_CONV_CH // (nh * DN_HD)
    return pl.pallas_call(
        functools.partial(_delta_kernel, nchunk=nchunk, nh=nh),
        grid=(DN_QK_HEADS, t // rb),
        in_specs=[pl.BlockSpec((rb, DN_HD), lambda p, i: (i, p)),
                  pl.BlockSpec((rb, DN_HD), lambda p, i: (i, kq + p)),
                  pl.BlockSpec((rb, nh * DN_HD), lambda p, i: (i, voff + p)),
                  pl.BlockSpec((rb, nh * DN_HD), lambda p, i: (i, zoff + p)),
                  pl.BlockSpec((rb, 2 * DN_V_HEADS), lambda p, i: (i, 0)),
                  pl.BlockSpec((nh, nchunk, 1, DN_C), lambda p, i: (p, i, 0, 0)),
                  pl.BlockSpec((1, DN_HD), lambda p, i: (0, 0))],
        out_specs=pl.BlockSpec((rb, nh * DN_HD), lambda p, i: (i, p)),
        out_shape=jax.ShapeDtypeStruct((t, DN_VAL_W), BF16),
        scratch_shapes=[pltpu.VMEM((nh, DN_HD, DN_HD), F32)],
        compiler_params=_cparams(("arbitrary", "arbitrary")),
    )(qkv, qkv, qkv, proj, bg, gc_rows, norm_w.reshape(1, DN_HD))


def _router_kernel(h_ref, w_ref, b_ref, idx_ref, wt_ref):
    logits = _dot3(h_ref[...], w_ref[...]) + b_ref[...]
    tm, e = logits.shape
    lane = lax.broadcasted_iota(jnp.int32, (tm, e), 1).astype(F32)
    lane_o = lax.broadcasted_iota(jnp.int32, (tm, LANE), 1)
    idx_out = jnp.zeros((tm, LANE), jnp.int32)
    val_out = jnp.zeros((tm, LANE), F32)
    cur = logits
    vals = []
    for kth in range(TOP_K):
        m = jnp.max(cur, axis=-1, keepdims=True)
        sel = jnp.min(jnp.where(cur == m, lane, float(e)), axis=-1, keepdims=True)
        cur = jnp.where(lane == sel, -jnp.inf, cur)
        idx_out = jnp.where(lane_o == kth, sel.astype(jnp.int32), idx_out)
        vals.append(m)
    es = [jnp.exp(v - vals[0]) for v in vals]
    tot = es[0]
    for x in es[1:]:
        tot = tot + x
    for kth in range(TOP_K):
        val_out = jnp.where(lane_o == kth, es[kth] / tot, val_out)
    idx_ref[...] = idx_out
    wt_ref[...] = val_out


def _router(h, w_router, b_router, tm=512):
    t, d = h.shape
    tm = min(tm, t)
    e = w_router.shape[1]
    return pl.pallas_call(
        _router_kernel,
        grid=(t // tm,),
        in_specs=[pl.BlockSpec((tm, d), lambda i: (i, 0)),
                  pl.BlockSpec((d, e), lambda i: (0, 0)),
                  pl.BlockSpec((1, e), lambda i: (0, 0))],
        out_specs=[pl.BlockSpec((tm, LANE), lambda i: (i, 0)),
                   pl.BlockSpec((tm, LANE), lambda i: (i, 0))],
        out_shape=[jax.ShapeDtypeStruct((t, LANE), jnp.int32),
                   jax.ShapeDtypeStruct((t, LANE), F32)],
        compiler_params=_cparams(("arbitrary",)),
    )(h, w_router, b_router.reshape(1, e))


def _expert_kernel(be_ref, na_ref, tok_ref, brun_ref, bnext_ref, h_ref, wg_ref, wu_ref, wd_ref,
                   bg_ref, bu_ref, bd_ref, o_ref, wgb_ref, wub_ref, wdb_ref, xbuf, sem,
                   wgf_ref, wuf_ref, wdf_ref, wsem, *, layer):
    i = pl.program_id(0)
    na = na_ref[0]
    bm = xbuf.shape[1] // ROW_TILES

    def weight_copies(e, s):
        return [pltpu.make_async_copy(src.at[layer, e], dst.at[s], wsem.at[s])
                for src, dst in ((wg_ref, wgf_ref), (wu_ref, wuf_ref), (wd_ref, wdf_ref))]

    @pl.when(i == 0)
    def _():
        for c in weight_copies(be_ref[0], 0):
            c.start()

    def row_copy(blk, slot, r):
        src = pl.multiple_of(tok_ref[blk * bm + r] * ROW_TILES, ROW_TILES)
        dst = pl.multiple_of(r * ROW_TILES, ROW_TILES)
        return pltpu.make_async_copy(h_ref.at[pl.ds(src, ROW_TILES)],
                                     xbuf.at[slot, pl.ds(dst, ROW_TILES)], sem.at[slot])

    def start_block(blk, slot):
        def body(r, carry):
            row_copy(blk, slot, r).start()
            return carry
        lax.fori_loop(0, bm, body, 0, unroll=8)

    nslot = xbuf.shape[0]
    ahead = nslot - 1

    @pl.when(i == 0)
    def _():
        start_block(0, 0)
        for b in range(1, ahead):
            @pl.when(b < na)
            def _():
                start_block(b, b)

    @pl.when(i + ahead < na)
    def _():
        start_block(i + ahead, (i + ahead) % nslot)

    prev = be_ref[jnp.maximum(i - 1, 0)]
    changed = jnp.logical_or(i == 0, be_ref[i] != prev)

    @pl.when(changed)
    def _():
        ws = brun_ref[i] % 2
        for c in weight_copies(be_ref[i], ws):
            c.wait()
        wgb_ref[...] = wgf_ref[ws].astype(BF16)
        wub_ref[...] = wuf_ref[ws].astype(BF16)
        wdb_ref[...] = wdf_ref[ws].astype(BF16)
        nxt = bnext_ref[i]

        @pl.when(nxt >= 0)
        def _():
            for c in weight_copies(nxt, 1 - ws):
                c.start()

    @pl.when(i < na)
    def _():
        slot = i % nslot

        def wait_body(r, carry):
            row_copy(i, slot, r).wait()
            return carry
        lax.fori_loop(0, bm, wait_body, 0, unroll=8)
        x = _load_token_rows(xbuf.at[slot], bm).astype(BF16)
        gate = jnp.minimum(jnp.dot(x, wgb_ref[...], preferred_element_type=F32) + bg_ref[...],
                           SWIGLU_LIMIT)
        up = jnp.clip(jnp.dot(x, wub_ref[...], preferred_element_type=F32) + bu_ref[...],
                      -SWIGLU_LIMIT, SWIGLU_LIMIT)
        hid = gate * _sigmoid(SWIGLU_ALPHA * gate) * (up + 1.0)
        y = jnp.dot(hid.astype(BF16), wdb_ref[...], preferred_element_type=F32) + bd_ref[...]
        _store_token_rows(o_ref, y)

    @pl.when(i >= na)
    def _():
        o_ref[...] = jnp.zeros_like(o_ref)


def _experts(h_rows, slot_tok, block_e, n_active, block_run, block_next, layer,
             w_gate, b_gate, w_up, b_up, w_down, b_down):
    d = D_MODEL
    n_slots = slot_tok.shape[0]
    nb = n_slots // MOE_BM
    f = w_gate.shape[-1]
    l = layer

    def bspec(shape):
        return pl.BlockSpec((None, None) + shape, lambda i, be, *_: (l, be[i], 0, 0))

    hbm = pl.BlockSpec(memory_space=pl.ANY)
    grid_spec = pltpu.PrefetchScalarGridSpec(
        num_scalar_prefetch=5,
        grid=(nb,),
        in_specs=[hbm, hbm, hbm, hbm, bspec((1, f)), bspec((1, f)), bspec((1, d))],
        out_specs=pl.BlockSpec((MOE_BM * ROW_TILES, LANE), lambda i, *_: (i, 0)),
        scratch_shapes=[pltpu.VMEM((d, f), BF16), pltpu.VMEM((d, f), BF16), pltpu.VMEM((f, d), BF16),
                        pltpu.VMEM((3, MOE_BM * ROW_TILES, LANE), U32), pltpu.SemaphoreType.DMA((3,)),
                        pltpu.VMEM((2, d, f), F32), pltpu.VMEM((2, d, f), F32), pltpu.VMEM((2, f, d), F32),
                        pltpu.SemaphoreType.DMA((2,))],
    )
    nl, ne = b_gate.shape[:2]
    return pl.pallas_call(
        functools.partial(_expert_kernel, layer=l),
        grid_spec=grid_spec,
        out_shape=jax.ShapeDtypeStruct((n_slots * ROW_TILES, LANE), U32),
        compiler_params=_cparams(("arbitrary",)),
    )(block_e, n_active, slot_tok, block_run, block_next, h_rows, w_gate, w_up, w_down,
      b_gate.reshape(nl, ne, 1, f), b_up.reshape(nl, ne, 1, f), b_down.reshape(nl, ne, 1, d))


def _routing(top_idx):
    t = top_idx.shape[0]
    sel = jnp.sum(jax.nn.one_hot(top_idx, N_EXPERTS, dtype=jnp.int32), axis=1)
    counts = jnp.sum(sel, axis=0)
    before = jnp.cumsum(sel, axis=0) - sel
    padded = (counts + MOE_BM - 1) // MOE_BM * MOE_BM
    pend = jnp.cumsum(padded)
    pstart = pend - padded
    dest = pstart[top_idx] + jnp.take_along_axis(before, top_idx, axis=1)
    n_blocks = t * TOP_K // MOE_BM + N_EXPERTS
    n_active = (pend[-1] // MOE_BM).astype(jnp.int32)
    blk = jnp.arange(n_blocks, dtype=jnp.int32)
    blk = jnp.minimum(blk, jnp.maximum(n_active - 1, 0))
    block_e = jnp.sum((blk[:, None] * MOE_BM >= pend[None, :]).astype(jnp.int32), axis=1)
    block_e = jnp.minimum(block_e, N_EXPERTS - 1).astype(jnp.int32)
    tok = jnp.broadcast_to(jnp.arange(t, dtype=jnp.int32)[:, None], dest.shape)
    slot_tok = jnp.zeros((n_blocks * MOE_BM,), jnp.int32).at[dest.reshape(-1)].set(tok.reshape(-1))
    present = (counts > 0)[None, :]
    ids = jnp.arange(N_EXPERTS, dtype=jnp.int32)[None, :]
    be = block_e[:, None]
    block_run = jnp.sum((present & (ids <= be)).astype(jnp.int32), axis=1) - 1
    block_next = jnp.min(jnp.where(present & (ids > be), ids, N_EXPERTS), axis=1)
    block_next = jnp.where(block_next >= N_EXPERTS, -1, block_next).astype(jnp.int32)
    return dest.astype(jnp.int32), slot_tok, block_e, n_active.reshape(1), block_run, block_next


def _combine_kernel(dest_ref, h_ref, w_ref, g_ref, b_ref, ys_ref, o_ref, obf_ref, buf, sem, *, tm):
    i = pl.program_id(0)
    n = pl.num_programs(0)

    def row_copy(blk, slot, r, k):
        src = pl.multiple_of(dest_ref[(blk * tm + r) * TOP_K + k] * ROW_TILES, ROW_TILES)
        dst = pl.multiple_of(r * ROW_TILES, ROW_TILES)
        return pltpu.make_async_copy(ys_ref.at[pl.ds(src, ROW_TILES)],
                                     buf.at[slot, k, pl.ds(dst, ROW_TILES)], sem.at[slot])

    def start_block(blk, slot):
        def body(r, carry):
            for k in range(TOP_K):
                row_copy(blk, slot, r, k).start()
            return carry
        lax.fori_loop(0, tm, body, 0, unroll=4)

    def wait_block(blk, slot):
        def body(r, carry):
            for k in range(TOP_K):
                row_copy(blk, slot, r, k).wait()
            return carry
        lax.fori_loop(0, tm, body, 0, unroll=4)

    nslot = buf.shape[0]
    ahead = nslot - 1

    @pl.when(i == 0)
    def _():
        start_block(0, 0)
        for b in range(1, ahead):
            @pl.when(b < n)
            def _():
                start_block(b, b)

    @pl.when(i + ahead < n)
    def _():
        start_block(i + ahead, (i + ahead) % nslot)

    slot = i % nslot
    wait_block(i, slot)
    w = w_ref[...]
    ffn = w[:, 0:1] * _load_token_rows(buf.at[slot, 0], tm)
    for k in range(1, TOP_K):
        ffn = ffn + w[:, k:k + 1] * _load_token_rows(buf.at[slot, k], tm)
    y = _layer_norm_rows(ALPHA * h_ref[...] + ffn, g_ref[...], b_ref[...])
    o_ref[...] = y
    obf_ref[...] = y.astype(BF16)


def _combine_norm(dest, h, top_w, ys, g, b, tm=128):
    t, d = h.shape
    tm = min(tm, t)
    row = pl.BlockSpec((tm, d), lambda i, dref: (i, 0))
    vec = pl.BlockSpec((1, d), lambda i, dref: (0, 0))
    grid_spec = pltpu.PrefetchScalarGridSpec(
        num_scalar_prefetch=1,
        grid=(t // tm,),
        in_specs=[row, pl.BlockSpec((tm, LANE), lambda i, dref: (i, 0)), vec, vec,
                  pl.BlockSpec(memory_space=pl.ANY)],
        out_specs=[row, row],
        scratch_shapes=[pltpu.VMEM((3, TOP_K, tm * ROW_TILES, LANE), U32),
                        pltpu.SemaphoreType.DMA((3,))],
    )
    return pl.pallas_call(
        functools.partial(_combine_kernel, tm=tm),
        grid_spec=grid_spec,
        out_shape=[jax.ShapeDtypeStruct((t, d), F32), jax.ShapeDtypeStruct((t, d), BF16)],
        compiler_params=_cparams(("arbitrary",)),
    )(dest.reshape(-1), h, top_w, g.reshape(1, d), b.reshape(1, d), ys)


def _layer(l, h, hb, p):
    t = h.shape[0]
    w_in_t = p['w_in_t']
    u = _matmul(hb, [(w_in_t, (l,), 0)], D_MODEL, F32, lambda acc: acc, tm=1024, tn=1024,
                w_rows_are_outputs=True)
    proj = _matmul(hb, [(w_in_t, (l,), OFF_QKV // 1024)], OFF_BETA - OFF_QKV, BF16, lambda acc: acc,
                   tm=1024, tn=1024, w_rows_are_outputs=True)
    gates = _matmul(hb, [(w_in_t[l, OFF_GATE_S5:], (), 0)], 2 * D_MODEL, BF16,
                    lambda acc: _sigmoid(acc), tm=1024, tn=1024, w_rows_are_outputs=True)
    bd = _matmul(hb, [(w_in_t[l, OFF_BETA:OFF_GATE_S5], (), 0)], 2 * DN_V_HEADS, F32,
                 lambda acc: acc, tn=2 * DN_V_HEADS, w_rows_are_outputs=True)

    y = _s5_apply(u, p['s5_tables'], l)
    part = _matmul(y, [(p['w_glu_a'], (l,), 0), (p['w_glu_b'], (l,), 0)], D_MODEL, BF16,
                   lambda a, b, g: a * _sigmoid(b) * g.astype(F32), extras=[(gates, 0)], tm=1024)

    qkv = _dn_conv(proj, p['dn_conv_w'][l].T)
    bg = _dn_gates(bd, p['dn_a_log'][l], p['dn_dt_bias'][l])
    gc_rows = bg[:, DN_V_HEADS:].T.reshape(DN_V_HEADS, t // DN_C, 1, DN_C)
    o = _delta_rule(qkv, proj, bg, gc_rows, p['dn_norm_w'][l])
    merged = _matmul(o, [(p['w_dn_out'], (l,), 0)], D_MODEL, BF16,
                     lambda acc, g, s: acc * g.astype(F32) + s.astype(F32),
                     extras=[(gates, D_MODEL // 512), (part, 0)], tm=1024)
    mix = _matmul(merged, [(p['w_mix_out'], (l,), 0)], D_MODEL, F32, lambda acc: acc, tm=1024, tn=1024)
    h, hb, h_rows = _deepnorm(h, mix, p['ln1_g'][l], p['ln1_b'][l])

    top_idx, top_w = _router(h, p['w_router'][l], p['b_router'][l])
    dest, slot_tok, block_e, n_active, block_run, block_next = _routing(top_idx[:, :TOP_K])
    ys = _experts(h_rows, slot_tok, block_e, n_active, block_run, block_next, l, p['w_gate'],
                  p['b_gate'], p['w_up'], p['b_up'], p['w_down'], p['b_down'])
    return _combine_norm(dest, h, top_w, ys, p['ln2_g'][l], p['ln2_b'][l])


def kernel(x, w_in, dn_conv_w, dn_a_log, dn_dt_bias, dn_norm_w, w_dn_out, s5_lam_re, s5_lam_im, s5_log_dt, s5_b_re, s5_b_im, s5_c_re, s5_c_im, s5_d, w_glu_a, w_glu_b, w_mix_out, ln1_g, ln1_b, w_router, b_router, w_gate, b_gate, w_up, b_up, w_down, b_down, ln2_g, ln2_b):
    p = dict(w_in_t=jnp.swapaxes(w_in, 1, 2), dn_conv_w=dn_conv_w, dn_a_log=dn_a_log, dn_dt_bias=dn_dt_bias,
             dn_norm_w=dn_norm_w, w_dn_out=w_dn_out, s5_lam_re=s5_lam_re, s5_lam_im=s5_lam_im,
             s5_log_dt=s5_log_dt, s5_b_re=s5_b_re, s5_b_im=s5_b_im, s5_c_re=s5_c_re,
             s5_c_im=s5_c_im, s5_d=s5_d, w_glu_a=w_glu_a, w_glu_b=w_glu_b, w_mix_out=w_mix_out,
             ln1_g=ln1_g, ln1_b=ln1_b, w_router=w_router, b_router=b_router, w_gate=w_gate,
             b_gate=b_gate, w_up=w_up, b_up=b_up, w_down=w_down, b_down=b_down,
             ln2_g=ln2_g, ln2_b=ln2_b)
    bsz, t, d = x.shape
    h = x.reshape(bsz * t, d)
    hb = h.astype(BF16)

    def groups(a):
        return a.reshape((-1,) + a.shape[2:])

    p['s5_tables'] = _s5_tables(groups(s5_lam_re), groups(s5_lam_im), groups(s5_log_dt), groups(s5_b_re),
                                groups(s5_b_im), groups(s5_c_re), groups(s5_c_im), groups(s5_d),
                                bsz * t // S5_L)
    for l in range(w_in.shape[0]):
        h, hb = _layer(l, h, hb, p)
    return h.reshape(bsz, t, d)
```

```python
import functools
import math

import jax
import jax.numpy as jnp
from jax import lax
from jax.experimental import pallas as pl
from jax.experimental.pallas import tpu as pltpu

F32 = jnp.float32
BF16 = jnp.bfloat16

D_MODEL = 2048
DEPTH = 4
S5_P = 16
S5_N = 64
S5_G = D_MODEL // S5_P
S5_L = 16
DN_QK_HEADS = 16
DN_V_HEADS = 32
DN_HD = 128
DN_KEY_W = DN_QK_HEADS * DN_HD
DN_VAL_W = DN_V_HEADS * DN_HD
DN_CONV_CH = 2 * DN_KEY_W + DN_VAL_W
DN_CONV = 4
DN_C = 128
N_EXPERTS = 32
TOP_K = 4
D_EXPERT = 512
SWIGLU_LIMIT = 7.0
SWIGLU_ALPHA = 1.702
MOE_BM = 256
ALPHA = (2 * DEPTH) ** 0.25
LN_EPS = 1e-5
RMS_EPS = 1e-6
L2_EPS = 1e-6
S5_MAX_REAL = -1e-4
OFF_QKV = D_MODEL
OFF_Z = OFF_QKV + DN_CONV_CH
OFF_BETA = OFF_Z + DN_VAL_W
OFF_GATE_S5 = OFF_BETA + 2 * DN_V_HEADS
IN_COLS = OFF_GATE_S5 + 2 * D_MODEL
LANE = 128
VMEM_LIMIT = 56 * 1024 * 1024


def _cparams(sem):
    return pltpu.CompilerParams(dimension_semantics=sem, vmem_limit_bytes=VMEM_LIMIT)


_NT = (((1,), (1,)), ((), ()))
_NN = (((1,), (0,)), ((), ()))


def _mm_kernel(*refs, nw, nx, epilogue, dims):
    a_ref = refs[0]
    w_refs = refs[1:1 + nw]
    x_refs = refs[1 + nw:1 + nw + nx]
    o_ref = refs[1 + nw + nx]
    wbf_refs = refs[2 + nw + nx:]

    @pl.when(pl.program_id(1) == 0)
    def _():
        for w_ref, wbf_ref in zip(w_refs, wbf_refs):
            wbf_ref[...] = w_ref[...].astype(BF16)

    a = a_ref[...].astype(BF16)
    accs = [lax.dot_general(a, wbf[...], dims, preferred_element_type=F32) for wbf in wbf_refs]
    o_ref[...] = epilogue(*accs, *[x[...] for x in x_refs]).astype(o_ref.dtype)


def _matmul(a, ws, n_out, out_dtype, epilogue, extras=(), tm=512, tn=512, w_rows_are_outputs=False):
    m, k = a.shape
    tm = min(tm, m)
    grid = (n_out // tn, m // tm)
    in_specs = [pl.BlockSpec((tm, k), lambda j, i: (i, 0))]
    args = [a]
    wshape = (tn, k) if w_rows_are_outputs else (k, tn)
    for w, lead, off in ws:
        nlead = len(lead)
        if w_rows_are_outputs:
            imap = lambda j, i, lead=lead, off=off: tuple(lead) + (j + off, 0)
        else:
            imap = lambda j, i, lead=lead, off=off: tuple(lead) + (0, j + off)
        in_specs.append(pl.BlockSpec((None,) * nlead + wshape, imap))
        args.append(w)
    for x, off in extras:
        in_specs.append(pl.BlockSpec((tm, tn), lambda j, i, off=off: (i, j + off)))
        args.append(x)
    return pl.pallas_call(
        functools.partial(_mm_kernel, nw=len(ws), nx=len(extras), epilogue=epilogue,
                          dims=_NT if w_rows_are_outputs else _NN),
        grid=grid,
        in_specs=in_specs,
        out_specs=pl.BlockSpec((tm, tn), lambda j, i: (i, j)),
        out_shape=jax.ShapeDtypeStruct((m, n_out), out_dtype),
        scratch_shapes=[pltpu.VMEM(wshape, BF16) for _ in ws],
        compiler_params=_cparams(("arbitrary", "arbitrary")),
    )(*args)


def _sigmoid(x):
    return 1.0 / (1.0 + jnp.exp(-x))


def _split_bf16(x):
    hi = x.astype(BF16)
    lo = (x - hi.astype(F32)).astype(BF16)
    return hi, lo


def _dot3(a, b, dims=_NN):
    ah, al = _split_bf16(a)
    bh, bl = _split_bf16(b)
    return (lax.dot_general(ah, bh, dims, preferred_element_type=F32)
            + lax.dot_general(ah, bl, dims, preferred_element_type=F32)
            + lax.dot_general(al, bh, dims, preferred_element_type=F32))


ROW_TILES = D_MODEL // (2 * LANE)
U32 = jnp.uint32


def _pack_pair(lo, hi):
    def rounded(x):
        b = lax.bitcast_convert_type(x, U32)
        return b + U32(0x7FFF) + ((b >> 16) & U32(1))
    return (rounded(lo) >> 16) | (rounded(hi) & U32(0xFFFF0000))


def _unpack_pair(u):
    return (lax.bitcast_convert_type(u << 16, F32),
            lax.bitcast_convert_type(u & U32(0xFFFF0000), F32))


def _store_token_rows(ref, y):
    n = y.shape[0]
    half = D_MODEL // 2
    for j in range(ROW_TILES):
        ref[pl.ds(j, n, stride=ROW_TILES), :] = _pack_pair(
            y[:, j * LANE:(j + 1) * LANE], y[:, half + j * LANE:half + (j + 1) * LANE])


def _load_token_rows(ref, n):
    pairs = [_unpack_pair(ref[pl.ds(j, n, stride=ROW_TILES), :]) for j in range(ROW_TILES)]
    return jnp.concatenate([p[0] for p in pairs] + [p[1] for p in pairs], axis=1)


def _layer_norm_rows(x, g, b):
    mu = jnp.mean(x, axis=-1, keepdims=True)
    xc = x - mu
    var = jnp.mean(xc * xc, axis=-1, keepdims=True)
    return xc * lax.rsqrt(var + LN_EPS) * g + b


def _ln_kernel(h_ref, r_ref, g_ref, b_ref, o_ref, obf_ref, orow_ref):
    y = _layer_norm_rows(ALPHA * h_ref[...] + r_ref[...], g_ref[...], b_ref[...])
    o_ref[...] = y
    obf_ref[...] = y.astype(BF16)
    _store_token_rows(orow_ref, y)


def _deepnorm(h, r, g, b, tm=256):
    t, d = h.shape
    tm = min(tm, t)
    row = pl.BlockSpec((tm, d), lambda i: (i, 0))
    vec = pl.BlockSpec((1, d), lambda i: (0, 0))
    return pl.pallas_call(
        _ln_kernel,
        grid=(t // tm,),
        in_specs=[row, row, vec, vec],
        out_specs=[row, row, pl.BlockSpec((tm * ROW_TILES, LANE), lambda i: (i, 0))],
        out_shape=[jax.ShapeDtypeStruct((t, d), F32), jax.ShapeDtypeStruct((t, d), BF16),
                   jax.ShapeDtypeStruct((t * ROW_TILES, LANE), U32)],
        compiler_params=_cparams(("arbitrary",)),
    )(h, r, g.reshape(1, d), b.reshape(1, d))


def _s5_tables(lam_re, lam_im, log_dt, b_re, b_im, c_re, c_im, d_skip, n_chunks):
    L = S5_L
    g = lam_re.shape[0]
    lre = jnp.minimum(lam_re, S5_MAX_REAL)
    lim = lam_im
    dt = jnp.exp(log_dt)[:, None]
    ks = jnp.arange(L + 1, dtype=F32)[:, None, None]
    mag = jnp.exp(lre * dt * ks)
    pr = mag * jnp.cos(lim * dt * ks)
    pi = mag * jnp.sin(lim * dt * ks)
    a_re, a_im = pr[1], pi[1]
    den = lre * lre + lim * lim
    f_re = ((a_re - 1.0) * lre + a_im * lim) / den
    f_im = (a_im * lre - (a_re - 1.0) * lim) / den
    bb_re = f_re[..., None] * b_re - f_im[..., None] * b_im
    bb_im = f_re[..., None] * b_im + f_im[..., None] * b_re
    bb = jnp.concatenate([jnp.transpose(bb_re, (0, 2, 1)), jnp.transpose(bb_im, (0, 2, 1))], axis=-1)
    cc = jnp.concatenate([c_re, c_im], axis=-1)
    qr = jnp.transpose(pr[L - 1::-1], (1, 0, 2))
    qi = jnp.transpose(pi[L - 1::-1], (1, 0, 2))
    p1 = jnp.concatenate([qr, qr], axis=-1)
    p2 = jnp.concatenate([-qi, qi], axis=-1)
    ur = jnp.transpose(pr[1:], (1, 0, 2))
    ui = jnp.transpose(pi[1:], (1, 0, 2))
    q1 = jnp.concatenate([ur, -ur], axis=-1)
    q2 = jnp.concatenate([-ui, -ui], axis=-1)
    nlev = max(1, int(math.log2(n_chunks)))
    mr, mi = pr[L], pi[L]
    m1, m2 = [], []
    for _ in range(nlev):
        m1.append(jnp.concatenate([mr, mr], axis=-1))
        m2.append(jnp.concatenate([-mi, mi], axis=-1))
        mr, mi = mr * mr - mi * mi, 2.0 * mr * mi
    pad = [jnp.zeros_like(m1[0])] * (16 - nlev)
    pw1 = jnp.stack(m1 + pad, axis=1)
    pw2 = jnp.stack(m2 + pad, axis=1)
    dsk = jnp.tile(d_skip, (1, L)).reshape(g, 1, L * S5_P)
    return p1, p2, q1, q2, bb, cc, pw1, pw2, dsk


def _gelu_tanh(y):
    return 0.5 * y * (1.0 + jnp.tanh(0.7978845608028654 * (y + 0.044715 * y * y * y)))


def _rep_rows(x, n):
    r, w = x.shape
    return jnp.broadcast_to(x[:, None, :], (r, n, w)).reshape(r * n, w)


def _tile_rows(x, n):
    r, w = x.shape
    return jnp.broadcast_to(x[None, :, :], (n, r, w)).reshape(n * r, w)


def _s5_kernel(u_ref, p1_ref, p2_ref, q1_ref, q2_ref, bb_ref, cc_ref, pw1_ref, pw2_ref, d_ref,
               o_ref, perm_ref, *, nlev, gb):
    L, P, N = S5_L, S5_P, S5_N
    lp = L * P
    half = (L // 2) * LANE
    nc = u_ref.shape[0] // L

    @pl.when(pl.program_id(0) == 0)
    def _():
        r = lax.broadcasted_iota(jnp.int32, (half, half), 0)
        c = lax.broadcasted_iota(jnp.int32, (half, half), 1)
        dst = ((r >> 4) & 7) * LANE + (r >> 7) * P + (r & 15)
        perm_ref[...] = jnp.where(c == dst, 1.0, 0.0).astype(BF16)

    perm = perm_ref[...]
    v = []
    for th in range(2):
        xs = [u_ref[pl.ds(th * 8 + tl, nc, stride=L), :].astype(BF16) for tl in range(8)]
        v.append(jnp.dot(jnp.concatenate(xs, axis=1), perm, preferred_element_type=F32).astype(BF16))

    row = lax.broadcasted_iota(jnp.int32, (nc, 2 * N), 0)
    lane_n = lax.broadcasted_iota(jnp.int32, (1, 2 * N), 1)
    sign = jnp.where(lane_n < N, 1.0, -1.0)
    rblk = lax.broadcasted_iota(jnp.int32, (lp, lp), 0) >> 4
    cblk = lax.broadcasted_iota(jnp.int32, (lp, lp), 1) >> 4
    z = [[], []]
    for gi in range(gb):
        ug = jnp.concatenate([v[0][:, gi * LANE:(gi + 1) * LANE],
                              v[1][:, gi * LANE:(gi + 1) * LANE]], axis=1)
        bbg = bb_ref[gi]
        ccg = cc_ref[gi]
        bmat = (_rep_rows(p1_ref[gi], P) * _tile_rows(bbg, L)
                + _rep_rows(p2_ref[gi], P) * _tile_rows(pltpu.roll(bbg, N, axis=1), L))
        cmt = (_rep_rows(q1_ref[gi], P) * _tile_rows(ccg, L)
               + _rep_rows(q2_ref[gi], P) * _tile_rows(pltpu.roll(ccg, N, axis=1), L))
        w = _dot3(bmat, _tile_rows(ccg * sign, L), _NT)
        tmat = jnp.zeros((lp, lp), F32)
        for t in range(L):
            shift = (lp - (L - 1 - t) * P) % lp
            tmat = jnp.where((cblk == t) & (rblk <= t), pltpu.roll(w, shift, axis=0), tmat)
        x = jnp.dot(ug, bmat.astype(BF16), preferred_element_type=F32)
        for lev in range(nlev):
            d = 1 << lev
            m1 = pw1_ref[gi, lev:lev + 1, :]
            m2 = pw2_ref[gi, lev:lev + 1, :]
            sh = jnp.where(row >= d, pltpu.roll(x, d, axis=0), 0.0)
            x = x + m1 * sh + m2 * pltpu.roll(sh, N, axis=1)
        sprev = jnp.where(row >= 1, pltpu.roll(x, 1, axis=0), 0.0)
        y = (jnp.dot(ug, tmat.astype(BF16), preferred_element_type=F32)
             + lax.dot_general(sprev.astype(BF16), cmt.astype(BF16), _NT, preferred_element_type=F32)
             + d_ref[gi] * ug.astype(F32))
        yg = _gelu_tanh(y).astype(BF16)
        z[0].append(yg[:, :LANE])
        z[1].append(yg[:, LANE:])
    for th in range(2):
        yp = lax.dot_general(jnp.concatenate(z[th], axis=1), perm, _NT, preferred_element_type=F32)
        for tl in range(8):
            o_ref[pl.ds(th * 8 + tl, nc, stride=L), :] = yp[:, tl * LANE:(tl + 1) * LANE]


def _s5_apply(u, tables, layer):
    t, d = u.shape
    gb = LANE // S5_P
    nc = t // S5_L
    nlev = int(math.log2(nc))
    assert (1 << nlev) == nc and nlev <= 16
    first = layer * (d // LANE)
    tab = pl.BlockSpec((gb, 16, 2 * S5_N), lambda i: (first + i, 0, 0))
    return pl.pallas_call(
        functools.partial(_s5_kernel, nlev=nlev, gb=gb),
        grid=(d // LANE,),
        in_specs=[pl.BlockSpec((t, LANE), lambda i: (0, i))] + [tab] * 8
                 + [pl.BlockSpec((gb, 1, S5_L * S5_P), lambda i: (first + i, 0, 0))],
        out_specs=pl.BlockSpec((t, LANE), lambda i: (0, i)),
        out_shape=jax.ShapeDtypeStruct((t, d), F32),
        scratch_shapes=[pltpu.VMEM((8 * LANE, 8 * LANE), BF16)],
        compiler_params=_cparams(("arbitrary",)),
    )(u, *tables)


def _conv_kernel(x_ref, p_ref, w_ref, o_ref, xs_ref, *, nq, nqk):
    j = pl.program_id(0)
    i = pl.program_id(1)
    tm, tc = x_ref.shape
    prev = p_ref[...].astype(F32)
    xs_ref[0:8, :] = jnp.where(i > 0, prev[8:16], 0.0)
    xs_ref[8:8 + tm, :] = x_ref[...].astype(F32)
    w = w_ref[...]
    acc = xs_ref[8:8 + tm, :] * w[DN_CONV - 1:DN_CONV, :]
    for s in range(1, DN_CONV):
        acc = acc + xs_ref[8 - s:8 - s + tm, :] * w[DN_CONV - 1 - s:DN_CONV - s, :]
    y = acc * _sigmoid(acc)
    @pl.when(j < nqk)
    def _():
        qscale = jnp.where(j < nq, DN_HD ** -0.5, 1.0)
        for hh in range(tc // DN_HD):
            blk = y[:, hh * DN_HD:(hh + 1) * DN_HD]
            ss = jnp.sum(blk * blk, axis=-1, keepdims=True)
            o_ref[:, hh * DN_HD:(hh + 1) * DN_HD] = (blk * (lax.rsqrt(ss + L2_EPS) * qscale)).astype(o_ref.dtype)

    @pl.when(j >= nqk)
    def _():
        o_ref[...] = y.astype(o_ref.dtype)


def _dn_conv(proj, conv_wt, tm=512, tc=512):
    t = proj.shape[0]
    tm = min(tm, t)
    pb = tm // 16
    return pl.pallas_call(
        functools.partial(_conv_kernel, nq=DN_KEY_W // tc, nqk=2 * DN_KEY_W // tc),
        grid=(DN_CONV_CH // tc, t // tm),
        in_specs=[pl.BlockSpec((tm, tc), lambda j, i: (i, j)),
                  pl.BlockSpec((16, tc), lambda j, i: (jnp.maximum(i * pb - 1, 0), j)),
                  pl.BlockSpec((DN_CONV, tc), lambda j, i: (0, j))],
        out_specs=pl.BlockSpec((tm, tc), lambda j, i: (i, j)),
        out_shape=jax.ShapeDtypeStruct((t, DN_CONV_CH), BF16),
        scratch_shapes=[pltpu.VMEM((tm + 8, tc), F32)],
        compiler_params=_cparams(("arbitrary", "arbitrary")),
    )(proj, proj, conv_wt)


def _dn_gate_kernel(x_ref, a_ref, dtb_ref, o_ref):
    x = x_ref[...]
    tm, w = x.shape
    lane = lax.broadcasted_iota(jnp.int32, (tm, w), 1)
    row = lax.broadcasted_iota(jnp.int32, (tm, w), 0)
    xs = x + dtb_ref[...]
    softplus = jnp.maximum(xs, 0.0) + jnp.log(1.0 + jnp.exp(-jnp.abs(xs)))
    g = -jnp.exp(a_ref[...]) * softplus
    pos = row & (DN_C - 1)
    d = 1
    while d < DN_C:
        g = g + jnp.where(pos >= d, pltpu.roll(g, d, axis=0), 0.0)
        d *= 2
    o_ref[...] = jnp.where(lane < DN_V_HEADS, _sigmoid(x), g)


def _dn_gates(bd, a_log, dt_bias, tm=512):
    t = bd.shape[0]
    tm = min(tm, t)
    zeros = jnp.zeros((DN_V_HEADS,), F32)
    a2 = jnp.concatenate([zeros, a_log]).reshape(1, -1)
    b2 = jnp.concatenate([zeros, dt_bias]).reshape(1, -1)
    w = 2 * DN_V_HEADS
    return pl.pallas_call(
        _dn_gate_kernel,
        grid=(t // tm,),
        in_specs=[pl.BlockSpec((tm, w), lambda i: (i, 0)),
                  pl.BlockSpec((1, w), lambda i: (0, 0)),
                  pl.BlockSpec((1, w), lambda i: (0, 0))],
        out_specs=pl.BlockSpec((tm, w), lambda i: (i, 0)),
        out_shape=jax.ShapeDtypeStruct((t, w), F32),
        compiler_params=_cparams(("arbitrary",)),
    )(bd, a2, b2)


def _bdot(a, b):
    return jnp.einsum('bij,bjk->bik', a.astype(BF16), b.astype(BF16), preferred_element_type=F32)


def _unit_lower_inverse(a, ii, jj):
    c = a.shape[-1]
    eye = (ii == jj).astype(F32)
    p = jnp.where((ii >> 3) == (jj >> 3), -a, 0.0)
    p2 = _bdot(p, p)
    t = eye + p
    t = t + _bdot(t, p2)
    p4 = _bdot(p2, p2)
    t = t + _bdot(t, p4)
    s = 8
    sh = 3
    while s < c:
        bi = ii >> sh
        bj = jj >> sh
        off = jnp.where(((bi & 1) == 1) & (bj == bi - 1), a, 0.0)
        t = t - _bdot(_bdot(t, off), t)
        s *= 2
        sh += 1
    return t


def _delta_kernel(q_ref, k_ref, v_ref, z_ref, bg_ref, gr_ref, nw_ref, o_ref, s_ref, *, nchunk, nh):
    pp = pl.program_id(0)
    i = pl.program_id(1)

    @pl.when(i == 0)
    def _():
        s_ref[...] = jnp.zeros_like(s_ref)

    c = DN_C
    hd = DN_HD
    bg = bg_ref[...]
    lane = lax.broadcasted_iota(jnp.int32, bg.shape, 1)
    ii = lax.broadcasted_iota(jnp.int32, (c, c), 0)
    jj = lax.broadcasted_iota(jnp.int32, (c, c), 1)
    nw = nw_ref[...]

    q3 = q_ref[...].reshape(nchunk, c, hd)
    k3 = k_ref[...].reshape(nchunk, c, hd)
    kk = jnp.einsum('cid,cjd->cij', k3, k3, preferred_element_type=F32)
    qk = jnp.einsum('cid,cjd->cij', q3, k3, preferred_element_type=F32)
    kf = k3.astype(F32)
    qf = q3.astype(F32)
    a_l, attn_l, rhs_l, qd_l, kd_l, gl_l = [], [], [], [], [], []
    for j in range(nh):
        hh = nh * pp + j
        beta = jnp.sum(jnp.where(lane == hh, bg, 0.0), axis=1, keepdims=True).reshape(nchunk, c, 1)
        gc = jnp.sum(jnp.where(lane == hh + DN_V_HEADS, bg, 0.0), axis=1,
                     keepdims=True).reshape(nchunk, c, 1)
        gr = gr_ref[j]
        decay = jnp.exp(jnp.where(ii >= jj, gc - gr, -jnp.inf))
        a_l.append(jnp.where(ii > jj, kk * decay, 0.0) * beta)
        attn_l.append((qk * decay).astype(BF16))
        eg = jnp.exp(gc)
        v = v_ref[:, j * hd:(j + 1) * hd].astype(F32).reshape(nchunk, c, hd)
        rhs_l.append(jnp.concatenate([v * beta, kf * (beta * eg)], axis=-1))
        qd_l.append((qf * eg).astype(BF16))
        g_last = gr[:, :, c - 1:c]
        kd_l.append(kf * jnp.exp(g_last - gc))
        gl_l.append(jnp.exp(g_last))
    tinv = _unit_lower_inverse(jnp.concatenate(a_l, axis=0), ii, jj)
    sol = _bdot(tinv, jnp.concatenate(rhs_l, axis=0))

    s = [s_ref[j] for j in range(nh)]
    for ci in range(nchunk):
        r0 = ci * c
        for j in range(nh):
            b = j * nchunk + ci
            u_c = sol[b, :, :hd]
            w_c = sol[b, :, hd:]
            sb = s[j].astype(BF16)
            v_new = u_c - jnp.dot(w_c.astype(BF16), sb, preferred_element_type=F32)
            vb = v_new.astype(BF16)
            o = (jnp.dot(qd_l[j][ci], sb, preferred_element_type=F32)
                 + jnp.dot(attn_l[j][ci], vb, preferred_element_type=F32))
            s[j] = s[j] * gl_l[j][ci] + jnp.dot(kd_l[j][ci].T.astype(BF16), vb,
                                               preferred_element_type=F32)
            ms = jnp.mean(o * o, axis=-1, keepdims=True)
            z = z_ref[r0:r0 + c, j * hd:(j + 1) * hd].astype(F32)
            o = o * lax.rsqrt(ms + RMS_EPS) * nw * (z * _sigmoid(z))
            o_ref[r0:r0 + c, j * hd:(j + 1) * hd] = o.astype(o_ref.dtype)
    for j in range(nh):
        s_ref[j] = s[j]


def _delta_rule(qkv, proj, bg, gc_rows, norm_w, rb=1024):
    t = qkv.shape[0]
    rb = min(rb, t)
    nchunk = rb // DN_C
    nh = DN_V_HEADS // DN_QK_HEADS
    kq = DN_KEY_W // DN_HD
    voff = 2 * DN_KEY_W // (nh * DN_HD)
    zoff = DN_CONV_CH // (nh * DN_HD)
    return pl.pallas_call(
        functools.partial(_delta_kernel, nchunk=nchunk, nh=nh),
        grid=(DN_QK_HEADS, t // rb),
        in_specs=[pl.BlockSpec((rb, DN_HD), lambda p, i: (i, p)),
                  pl.BlockSpec((rb, DN_HD), lambda p, i: (i, kq + p)),
                  pl.BlockSpec((rb, nh * DN_HD), lambda p, i: (i, voff + p)),
                  pl.BlockSpec((rb, nh * DN_HD), lambda p, i: (i, zoff + p)),
                  pl.BlockSpec((rb, 2 * DN_V_HEADS), lambda p, i: (i, 0)),
                  pl.BlockSpec((nh, nchunk, 1, DN_C), lambda p, i: (p, i, 0, 0)),
                  pl.BlockSpec((1, DN_HD), lambda p, i: (0, 0))],
        out_specs=pl.BlockSpec((rb, nh * DN_HD), lambda p, i: (i, p)),
        out_shape=jax.ShapeDtypeStruct((t, DN_VAL_W), BF16),
        scratch_shapes=[pltpu.VMEM((nh, DN_HD, DN_HD), F32)],
        compiler_params=_cparams(("arbitrary", "arbitrary")),
    )(qkv, qkv, qkv, proj, bg, gc_rows, norm_w.reshape(1, DN_HD))


def _router_kernel(h_ref, w_ref, b_ref, idx_ref, wt_ref):
    logits = _dot3(h_ref[...], w_ref[...]) + b_ref[...]
    tm, e = logits.shape
    lane = lax.broadcasted_iota(jnp.int32, (tm, e), 1).astype(F32)
    lane_o = lax.broadcasted_iota(jnp.int32, (tm, LANE), 1)
    idx_out = jnp.zeros((tm, LANE), jnp.int32)
    val_out = jnp.zeros((tm, LANE), F32)
    cur = logits
    vals = []
    for kth in range(TOP_K):
        m = jnp.max(cur, axis=-1, keepdims=True)
        sel = jnp.min(jnp.where(cur == m, lane, float(e)), axis=-1, keepdims=True)
        cur = jnp.where(lane == sel, -jnp.inf, cur)
        idx_out = jnp.where(lane_o == kth, sel.astype(jnp.int32), idx_out)
        vals.append(m)
    es = [jnp.exp(v - vals[0]) for v in vals]
    tot = es[0]
    for x in es[1:]:
        tot = tot + x
    for kth in range(TOP_K):
        val_out = jnp.where(lane_o == kth, es[kth] / tot, val_out)
    idx_ref[...] = idx_out
    wt_ref[...] = val_out


def _router(h, w_router, b_router, tm=512):
    t, d = h.shape
    tm = min(tm, t)
    e = w_router.shape[1]
    return pl.pallas_call(
        _router_kernel,
        grid=(t // tm,),
        in_specs=[pl.BlockSpec((tm, d), lambda i: (i, 0)),
                  pl.BlockSpec((d, e), lambda i: (0, 0)),
                  pl.BlockSpec((1, e), lambda i: (0, 0))],
        out_specs=[pl.BlockSpec((tm, LANE), lambda i: (i, 0)),
                   pl.BlockSpec((tm, LANE), lambda i: (i, 0))],
        out_shape=[jax.ShapeDtypeStruct((t, LANE), jnp.int32),
                   jax.ShapeDtypeStruct((t, LANE), F32)],
        compiler_params=_cparams(("arbitrary",)),
    )(h, w_router, b_router.reshape(1, e))


def _expert_kernel(be_ref, na_ref, tok_ref, brun_ref, bnext_ref, h_ref, wg_ref, wu_ref, wd_ref,
                   bg_ref, bu_ref, bd_ref, o_ref, wgb_ref, wub_ref, wdb_ref, xbuf, sem,
                   wgf_ref, wuf_ref, wdf_ref, wsem, *, layer):
    i = pl.program_id(0)
    na = na_ref[0]
    bm = xbuf.shape[1] // ROW_TILES

    def weight_copies(e, s):
        return [pltpu.make_async_copy(src.at[layer, e], dst.at[s], wsem.at[s])
                for src, dst in ((wg_ref, wgf_ref), (wu_ref, wuf_ref), (wd_ref, wdf_ref))]

    @pl.when(i == 0)
    def _():
        for c in weight_copies(be_ref[0], 0):
            c.start()

    def row_copy(blk, slot, r):
        src = pl.multiple_of(tok_ref[blk * bm + r] * ROW_TILES, ROW_TILES)
        dst = pl.multiple_of(r * ROW_TILES, ROW_TILES)
        return pltpu.make_async_copy(h_ref.at[pl.ds(src, ROW_TILES)],
                                     xbuf.at[slot, pl.ds(dst, ROW_TILES)], sem.at[slot])

    def start_block(blk, slot):
        def body(r, carry):
            row_copy(blk, slot, r).start()
            return carry
        lax.fori_loop(0, bm, body, 0, unroll=8)

    nslot = xbuf.shape[0]
    ahead = nslot - 1

    @pl.when(i == 0)
    def _():
        start_block(0, 0)
        for b in range(1, ahead):
            @pl.when(b < na)
            def _():
                start_block(b, b)

    @pl.when(i + ahead < na)
    def _():
        start_block(i + ahead, (i + ahead) % nslot)

    prev = be_ref[jnp.maximum(i - 1, 0)]
    changed = jnp.logical_or(i == 0, be_ref[i] != prev)

    @pl.when(changed)
    def _():
        ws = brun_ref[i] % 2
        for c in weight_copies(be_ref[i], ws):
            c.wait()
        wgb_ref[...] = wgf_ref[ws].astype(BF16)
        wub_ref[...] = wuf_ref[ws].astype(BF16)
        wdb_ref[...] = wdf_ref[ws].astype(BF16)
        nxt = bnext_ref[i]

        @pl.when(nxt >= 0)
        def _():
            for c in weight_copies(nxt, 1 - ws):
                c.start()

    @pl.when(i < na)
    def _():
        slot = i % nslot

        def wait_body(r, carry):
            row_copy(i, slot, r).wait()
            return carry
        lax.fori_loop(0, bm, wait_body, 0, unroll=8)
        x = _load_token_rows(xbuf.at[slot], bm).astype(BF16)
        gate = jnp.minimum(jnp.dot(x, wgb_ref[...], preferred_element_type=F32) + bg_ref[...],
                           SWIGLU_LIMIT)
        up = jnp.clip(jnp.dot(x, wub_ref[...], preferred_element_type=F32) + bu_ref[...],
                      -SWIGLU_LIMIT, SWIGLU_LIMIT)
        hid = gate * _sigmoid(SWIGLU_ALPHA * gate) * (up + 1.0)
        y = jnp.dot(hid.astype(BF16), wdb_ref[...], preferred_element_type=F32) + bd_ref[...]
        _store_token_rows(o_ref, y)

    @pl.when(i >= na)
    def _():
        o_ref[...] = jnp.zeros_like(o_ref)


def _experts(h_rows, slot_tok, block_e, n_active, block_run, block_next, layer,
             w_gate, b_gate, w_up, b_up, w_down, b_down):
    d = D_MODEL
    n_slots = slot_tok.shape[0]
    nb = n_slots // MOE_BM
    f = w_gate.shape[-1]
    l = layer

    def bspec(shape):
        return pl.BlockSpec((None, None) + shape, lambda i, be, *_: (l, be[i], 0, 0))

    hbm = pl.BlockSpec(memory_space=pl.ANY)
    grid_spec = pltpu.PrefetchScalarGridSpec(
        num_scalar_prefetch=5,
        grid=(nb,),
        in_specs=[hbm, hbm, hbm, hbm, bspec((1, f)), bspec((1, f)), bspec((1, d))],
        out_specs=pl.BlockSpec((MOE_BM * ROW_TILES, LANE), lambda i, *_: (i, 0)),
        scratch_shapes=[pltpu.VMEM((d, f), BF16), pltpu.VMEM((d, f), BF16), pltpu.VMEM((f, d), BF16),
                        pltpu.VMEM((3, MOE_BM * ROW_TILES, LANE), U32), pltpu.SemaphoreType.DMA((3,)),
                        pltpu.VMEM((2, d, f), F32), pltpu.VMEM((2, d, f), F32), pltpu.VMEM((2, f, d), F32),
                        pltpu.SemaphoreType.DMA((2,))],
    )
    nl, ne = b_gate.shape[:2]
    return pl.pallas_call(
        functools.partial(_expert_kernel, layer=l),
        grid_spec=grid_spec,
        out_shape=jax.ShapeDtypeStruct((n_slots * ROW_TILES, LANE), U32),
        compiler_params=_cparams(("arbitrary",)),
    )(block_e, n_active, slot_tok, block_run, block_next, h_rows, w_gate, w_up, w_down,
      b_gate.reshape(nl, ne, 1, f), b_up.reshape(nl, ne, 1, f), b_down.reshape(nl, ne, 1, d))


def _routing(top_idx):
    t = top_idx.shape[0]
    sel = jnp.sum(jax.nn.one_hot(top_idx, N_EXPERTS, dtype=jnp.int32), axis=1)
    counts = jnp.sum(sel, axis=0)
    before = jnp.cumsum(sel, axis=0) - sel
    padded = (counts + MOE_BM - 1) // MOE_BM * MOE_BM
    pend = jnp.cumsum(padded)
    pstart = pend - padded
    dest = pstart[top_idx] + jnp.take_along_axis(before, top_idx, axis=1)
    n_blocks = t * TOP_K // MOE_BM + N_EXPERTS
    n_active = (pend[-1] // MOE_BM).astype(jnp.int32)
    blk = jnp.arange(n_blocks, dtype=jnp.int32)
    blk = jnp.minimum(blk, jnp.maximum(n_active - 1, 0))
    block_e = jnp.sum((blk[:, None] * MOE_BM >= pend[None, :]).astype(jnp.int32), axis=1)
    block_e = jnp.minimum(block_e, N_EXPERTS - 1).astype(jnp.int32)
    tok = jnp.broadcast_to(jnp.arange(t, dtype=jnp.int32)[:, None], dest.shape)
    slot_tok = jnp.zeros((n_blocks * MOE_BM,), jnp.int32).at[dest.reshape(-1)].set(
        tok.reshape(-1), unique_indices=True, mode='promise_in_bounds')
    present = (counts > 0)[None, :]
    ids = jnp.arange(N_EXPERTS, dtype=jnp.int32)[None, :]
    be = block_e[:, None]
    block_run = jnp.sum((present & (ids <= be)).astype(jnp.int32), axis=1) - 1
    block_next = jnp.min(jnp.where(present & (ids > be), ids, N_EXPERTS), axis=1)
    block_next = jnp.where(block_next >= N_EXPERTS, -1, block_next).astype(jnp.int32)
    return dest.astype(jnp.int32), slot_tok, block_e, n_active.reshape(1), block_run, block_next


def _combine_kernel(dest_ref, h_ref, w_ref, g_ref, b_ref, ys_ref, o_ref, obf_ref, buf, sem, *, tm):
    i = pl.program_id(0)
    n = pl.num_programs(0)

    def row_copy(blk, slot, r, k):
        src = pl.multiple_of(dest_ref[(blk * tm + r) * TOP_K + k] * ROW_TILES, ROW_TILES)
        dst = pl.multiple_of(r * ROW_TILES, ROW_TILES)
        return pltpu.make_async_copy(ys_ref.at[pl.ds(src, ROW_TILES)],
                                     buf.at[slot, k, pl.ds(dst, ROW_TILES)], sem.at[slot])

    def start_block(blk, slot):
        def body(r, carry):
            for k in range(TOP_K):
                row_copy(blk, slot, r, k).start()
            return carry
        lax.fori_loop(0, tm, body, 0, unroll=4)

    def wait_block(blk, slot):
        def body(r, carry):
            for k in range(TOP_K):
                row_copy(blk, slot, r, k).wait()
            return carry
        lax.fori_loop(0, tm, body, 0, unroll=4)

    nslot = buf.shape[0]
    ahead = nslot - 1

    @pl.when(i == 0)
    def _():
        start_block(0, 0)
        for b in range(1, ahead):
            @pl.when(b < n)
            def _():
                start_block(b, b)

    @pl.when(i + ahead < n)
    def _():
        start_block(i + ahead, (i + ahead) % nslot)

    slot = i % nslot
    wait_block(i, slot)
    w = w_ref[...]
    ffn = w[:, 0:1] * _load_token_rows(buf.at[slot, 0], tm)
    for k in range(1, TOP_K):
        ffn = ffn + w[:, k:k + 1] * _load_token_rows(buf.at[slot, k], tm)
    y = _layer_norm_rows(ALPHA * h_ref[...] + ffn, g_ref[...], b_ref[...])
    o_ref[...] = y
    obf_ref[...] = y.astype(BF16)


def _combine_norm(dest, h, top_w, ys, g, b, tm=128):
    t, d = h.shape
    tm = min(tm, t)
    row = pl.BlockSpec((tm, d), lambda i, dref: (i, 0))
    vec = pl.BlockSpec((1, d), lambda i, dref: (0, 0))
    grid_spec = pltpu.PrefetchScalarGridSpec(
        num_scalar_prefetch=1,
        grid=(t // tm,),
        in_specs=[row, pl.BlockSpec((tm, LANE), lambda i, dref: (i, 0)), vec, vec,
                  pl.BlockSpec(memory_space=pl.ANY)],
        out_specs=[row, row],
        scratch_shapes=[pltpu.VMEM((3, TOP_K, tm * ROW_TILES, LANE), U32),
                        pltpu.SemaphoreType.DMA((3,))],
    )
    return pl.pallas_call(
        functools.partial(_combine_kernel, tm=tm),
        grid_spec=grid_spec,
        out_shape=[jax.ShapeDtypeStruct((t, d), F32), jax.ShapeDtypeStruct((t, d), BF16)],
        compiler_params=_cparams(("arbitrary",)),
    )(dest.reshape(-1), h, top_w, g.reshape(1, d), b.reshape(1, d), ys)


def _layer(l, h, hb, p):
    t = h.shape[0]
    w_in_t = p['w_in_t']
    u = _matmul(hb, [(w_in_t, (l,), 0)], D_MODEL, F32, lambda acc: acc, tm=1024, tn=1024,
                w_rows_are_outputs=True)
    proj = _matmul(hb, [(w_in_t, (l,), OFF_QKV // 1024)], OFF_BETA - OFF_QKV, BF16, lambda acc: acc,
                   tm=1024, tn=1024, w_rows_are_outputs=True)
    gates = _matmul(hb, [(w_in_t[l, OFF_GATE_S5:], (), 0)], 2 * D_MODEL, BF16,
                    lambda acc: _sigmoid(acc), tm=1024, tn=1024, w_rows_are_outputs=True)
    bd = _matmul(hb, [(w_in_t[l, OFF_BETA:OFF_GATE_S5], (), 0)], 2 * DN_V_HEADS, F32,
                 lambda acc: acc, tn=2 * DN_V_HEADS, w_rows_are_outputs=True)

    y = _s5_apply(u, p['s5_tables'], l)
    part = _matmul(y, [(p['w_glu_a'], (l,), 0), (p['w_glu_b'], (l,), 0)], D_MODEL, BF16,
                   lambda a, b, g: a * _sigmoid(b) * g.astype(F32), extras=[(gates, 0)], tm=1024)

    qkv = _dn_conv(proj, p['dn_conv_w'][l].T)
    bg = _dn_gates(bd, p['dn_a_log'][l], p['dn_dt_bias'][l])
    gc_rows = bg[:, DN_V_HEADS:].T.reshape(DN_V_HEADS, t // DN_C, 1, DN_C)
    o = _delta_rule(qkv, proj, bg, gc_rows, p['dn_norm_w'][l])
    merged = _matmul(o, [(p['w_dn_out'], (l,), 0)], D_MODEL, BF16,
                     lambda acc, g, s: acc * g.astype(F32) + s.astype(F32),
                     extras=[(gates, D_MODEL // 512), (part, 0)], tm=1024)
    mix = _matmul(merged, [(p['w_mix_out'], (l,), 0)], D_MODEL, F32, lambda acc: acc, tm=1024, tn=1024)
    h, hb, h_rows = _deepnorm(h, mix, p['ln1_g'][l], p['ln1_b'][l])

    top_idx, top_w = _router(h, p['w_router'][l], p['b_router'][l])
    dest, slot_tok, block_e, n_active, block_run, block_next = _routing(top_idx[:, :TOP_K])
    ys = _experts(h_rows, slot_tok, block_e, n_active, block_run, block_next, l, p['w_gate'],
                  p['b_gate'], p['w_up'], p['b_up'], p['w_down'], p['b_down'])
    return _combine_norm(dest, h, top_w, ys, p['ln2_g'][l], p['ln2_b'][l])


def kernel(x, w_in, dn_conv_w, dn_a_log, dn_dt_bias, dn_norm_w, w_dn_out, s5_lam_re, s5_lam_im, s5_log_dt, s5_b_re, s5_b_im, s5_c_re, s5_c_im, s5_d, w_glu_a, w_glu_b, w_mix_out, ln1_g, ln1_b, w_router, b_router, w_gate, b_gate, w_up, b_up, w_down, b_down, ln2_g, ln2_b):
    p = dict(w_in_t=jnp.swapaxes(w_in, 1, 2), dn_conv_w=dn_conv_w, dn_a_log=dn_a_log, dn_dt_bias=dn_dt_bias,
             dn_norm_w=dn_norm_w, w_dn_out=w_dn_out, s5_lam_re=s5_lam_re, s5_lam_im=s5_lam_im,
             s5_log_dt=s5_log_dt, s5_b_re=s5_b_re, s5_b_im=s5_b_im, s5_c_re=s5_c_re,
             s5_c_im=s5_c_im, s5_d=s5_d, w_glu_a=w_glu_a, w_glu_b=w_glu_b, w_mix_out=w_mix_out,
             ln1_g=ln1_g, ln1_b=ln1_b, w_router=w_router, b_router=b_router, w_gate=w_gate,
             b_gate=b_gate, w_up=w_up, b_up=b_up, w_down=w_down, b_down=b_down,
             ln2_g=ln2_g, ln2_b=ln2_b)
    bsz, t, d = x.shape
    h = x.reshape(bsz * t, d)
    hb = h.astype(BF16)

    def groups(a):
        return a.reshape((-1,) + a.shape[2:])

    p['s5_tables'] = _s5_tables(groups(s5_lam_re), groups(s5_lam_im), groups(s5_log_dt), groups(s5_b_re),
                                groups(s5_b_im), groups(s5_c_re), groups(s5_c_im), groups(s5_d),
                                bsz * t // S5_L)
    for l in range(w_in.shape[0]):
        h, hb = _layer(l, h, hb, p)
    return h.reshape(bsz, t, d)
```
